```python
import math
import jax, jax.numpy as jnp
from jax import lax
import numpy as np

D_MODEL = 1024
BATCH = 32
SEQ = 2048
DEPTH = 1

PLE_DIM = 256
CONV_WIDTH = 1024
CONV_K = 3
N_HEADS = 8
HEAD_DIM = 64
V_DIM = 2 * HEAD_DIM
ATTN_WIDTH = N_HEADS * V_DIM
QK_WIDTH = N_HEADS * 2 * HEAD_DIM
Q_BLOCK = 128
LN_EPS = 1e-5
RMS_EPS = 1e-5

BRANCH_COLS = (CONV_WIDTH, CONV_WIDTH, CONV_WIDTH, CONV_WIDTH,
               QK_WIDTH, QK_WIDTH, ATTN_WIDTH, ATTN_WIDTH,
               D_MODEL, D_MODEL)
TOTAL_COLS = sum(BRANCH_COLS)
SPLIT_POINTS = tuple(int(s) for s in np.cumsum(BRANCH_COLS)[:-1])

kernel_name = "hybrid_shortconv_diffattn_deepnorm_encoder"


def alibi_slopes():
    return jnp.asarray(np.float32(2.0) ** (-8.0 * np.arange(1, N_HEADS + 1, dtype=np.float32) / N_HEADS))


def layer_norm(x, g, b):
    xf = x.astype(jnp.float32)
    mu = jnp.mean(xf, axis=-1, keepdims=True)
    xc = xf - mu
    var = jnp.mean(xc * xc, axis=-1, keepdims=True)
    y = xc * lax.rsqrt(var + LN_EPS) * g.astype(jnp.float32) + b.astype(jnp.float32)
    return y.astype(x.dtype)


def rms_norm(x, g):
    xf = x.astype(jnp.float32)
    y = xf * lax.rsqrt(jnp.mean(xf * xf, axis=-1, keepdims=True) + RMS_EPS) * g.astype(jnp.float32)
    return y.astype(x.dtype)


def short_conv_branch(u, c, bgate, z, conv_w, conv_b, w_proj):
    h = c * u
    h = lax.conv_general_dilated(
        h, conv_w[:, None, :], window_strides=(1,),
        padding=[(CONV_K // 2, CONV_K // 2)],
        dimension_numbers=("NWC", "WIO", "NWC"),
        feature_group_count=CONV_WIDTH) + conv_b
    y = bgate * h * jax.nn.silu(z)
    return y @ w_proj


def diff_attention_branch(q, k, v, z, lq1, lk1, lq2, lk2, subln_g, w_proj, slopes, lambda_init):
    bsz, seq, _ = q.shape
    q = q.reshape(bsz, seq, N_HEADS, 2, HEAD_DIM) * (HEAD_DIM ** -0.5)
    k = k.reshape(bsz, seq, N_HEADS, 2, HEAD_DIM)
    v = v.reshape(bsz, seq, N_HEADS, V_DIM)
    lam = (jnp.exp(jnp.sum(lq1.astype(jnp.float32) * lk1.astype(jnp.float32)))
           - jnp.exp(jnp.sum(lq2.astype(jnp.float32) * lk2.astype(jnp.float32)))
           + lambda_init)
    n_blocks = seq // Q_BLOCK
    q_blocks = q.reshape(bsz, n_blocks, Q_BLOCK, N_HEADS, 2, HEAD_DIM).transpose(1, 0, 2, 3, 4, 5)
    q_pos = jnp.arange(seq, dtype=jnp.int32).reshape(n_blocks, Q_BLOCK)
    k_pos = jnp.arange(seq, dtype=jnp.int32)

    def attend(args):
        qb, qp = args
        logits = jnp.einsum("bqhmd,bkhmd->bhmqk", qb, k).astype(jnp.float32)
        dist = jnp.abs(qp[:, None] - k_pos[None, :]).astype(jnp.float32)
        logits = logits - slopes[:, None, None, None] * dist
        probs = jax.nn.softmax(logits, axis=-1)
        weights = probs[:, :, 0] - lam * probs[:, :, 1]
        return jnp.einsum("bhqk,bkhe->bqhe", weights.astype(v.dtype), v)

    o = lax.map(attend, (q_blocks, q_pos))
    o = o.transpose(1, 0, 2, 3, 4).reshape(bsz, seq, N_HEADS, V_DIM)
    o = rms_norm(o, subln_g) * (1.0 - lambda_init)
    o = o.reshape(bsz, seq, ATTN_WIDTH) * jax.nn.silu(z)
    return o @ w_proj


def setup_inputs(seed: int = 0) -> dict:
    key = jax.random.key(seed)
    ks = jax.random.split(key, 17)
    beta = (8.0 * DEPTH) ** -0.25
    f32 = jnp.float32
    col_scale = jnp.concatenate([
        jnp.full((n,), s, dtype=f32) for n, s in zip(
            BRANCH_COLS, (beta, 1.0, 1.0, 1.0, 1.0, 1.0, beta, 1.0, 1.0, 1.0))])
    x = jax.random.normal(ks[0], (BATCH, SEQ, D_MODEL), f32)
    p = jax.random.normal(ks[1], (DEPTH, BATCH, SEQ, PLE_DIM), f32)
    w_in = jax.random.normal(ks[2], (DEPTH, D_MODEL, TOTAL_COLS), f32) * (D_MODEL ** -0.5) * col_scale
    conv_w = jax.random.normal(ks[3], (DEPTH, CONV_K, CONV_WIDTH), f32) * (CONV_K ** -0.5)
    conv_b = 0.01 * jax.random.normal(ks[4], (DEPTH, CONV_WIDTH), f32)
    w_proj_a = jax.random.normal(ks[5], (DEPTH, CONV_WIDTH, D_MODEL), f32) * (CONV_WIDTH ** -0.5) * beta
    lambda_q1 = 0.1 * jax.random.normal(ks[6], (DEPTH, HEAD_DIM), f32)
    lambda_k1 = 0.1 * jax.random.normal(ks[7], (DEPTH, HEAD_DIM), f32)
    lambda_q2 = 0.1 * jax.random.normal(ks[8], (DEPTH, HEAD_DIM), f32)
    lambda_k2 = 0.1 * jax.random.normal(ks[9], (DEPTH, HEAD_DIM), f32)
    subln_g = 1.0 + 0.01 * jax.random.normal(ks[10], (DEPTH, V_DIM), f32)
    w_proj_b = jax.random.normal(ks[11], (DEPTH, ATTN_WIDTH, D_MODEL), f32) * (ATTN_WIDTH ** -0.5) * beta
    w_out = jax.random.normal(ks[12], (DEPTH, D_MODEL, D_MODEL), f32) * (D_MODEL ** -0.5) * beta
    w_ple = jax.random.normal(ks[13], (DEPTH, PLE_DIM, D_MODEL), f32) * (PLE_DIM ** -0.5)
    w_ple_gate = jax.random.normal(ks[14], (DEPTH, D_MODEL, D_MODEL), f32) * (D_MODEL ** -0.5)
    ln_g = 1.0 + 0.01 * jax.random.normal(ks[15], (DEPTH, D_MODEL), f32)
    ln_b = 0.01 * jax.random.normal(ks[16], (DEPTH, D_MODEL), f32)
    return {"x": x, "p": p, "w_in": w_in, "conv_w": conv_w, "conv_b": conv_b,
            "w_proj_a": w_proj_a, "lambda_q1": lambda_q1, "lambda_k1": lambda_k1,
            "lambda_q2": lambda_q2, "lambda_k2": lambda_k2, "subln_g": subln_g,
            "w_proj_b": w_proj_b, "w_out": w_out, "w_ple": w_ple,
            "w_ple_gate": w_ple_gate, "ln_g": ln_g, "ln_b": ln_b}


def reference(x, p, w_in, conv_w, conv_b, w_proj_a, lambda_q1, lambda_k1, lambda_q2,
              lambda_k2, subln_g, w_proj_b, w_out, w_ple, w_ple_gate, ln_g, ln_b):
    alpha = (2.0 * DEPTH) ** 0.25
    slopes = alibi_slopes()
    h = x
    for i in range(DEPTH):
        lambda_init = 0.8 - 0.6 * math.exp(-0.3 * i)
        proj = h @ w_in[i]
        u, c, bg, za, q, k, v, zb, ga, gb = jnp.split(proj, SPLIT_POINTS, axis=-1)
        y_a = short_conv_branch(u, c, bg, za, conv_w[i], conv_b[i], w_proj_a[i])
        y_b = diff_attention_branch(q, k, v, zb, lambda_q1[i], lambda_k1[i], lambda_q2[i],
                                    lambda_k2[i], subln_g[i], w_proj_b[i], slopes, lambda_init)
        merged = jax.nn.sigmoid(ga) * y_a + jax.nn.sigmoid(gb) * y_b
        r = alpha * h + merged @ w_out[i]
        r = r + jax.nn.sigmoid(r @ w_ple_gate[i]) * (p[i] @ w_ple[i])
        h = layer_norm(r, ln_g[i], ln_b[i])
    return h
```

```python
import functools
import math

import jax
import jax.numpy as jnp
import numpy as np
from jax import lax
from jax.experimental import pallas as pl
from jax.experimental.pallas import tpu as pltpu

N_HEADS = 8
HEAD_DIM = 64
V_DIM = 2 * HEAD_DIM
LN_EPS = 1e-5
RMS_EPS = 1e-5
CONV_K = 3
LOG2E = math.log2(math.e)

VMEM_LIMIT_BYTES = 56 * 1024 * 1024

CONV_CB = 256
ATTN_TQ = 256
ATTN_RB = 16
EPI_T = 512

BF16 = jnp.bfloat16
F32 = jnp.float32


def _dot(a, b):
    return jnp.dot(a, b, preferred_element_type=F32)


def _sigmoid(x):
    return 1.0 / (1.0 + jnp.exp(-x))


def _silu(x):
    return x * _sigmoid(x)


def _conv_branch_kernel(x_ref, wu_ref, wc_ref, wb_ref, wz_ref, cw_ref, cb_ref,
                        wpa_ref, ya_ref, xb_scr, y_scr):
    cb = pl.program_id(1)
    n_cb = pl.num_programs(1)
    seq = x_ref.shape[0]

    @pl.when(cb == 0)
    def _():
        xb_scr[...] = x_ref[...].astype(BF16)

    xb = xb_scr[...]
    h = _dot(xb, wc_ref[...]) * _dot(xb, wu_ref[...])
    row = lax.broadcasted_iota(jnp.int32, h.shape, 0)
    h_prev = jnp.where(row == 0, 0.0, pltpu.roll(h, 1, axis=0))
    h_next = jnp.where(row == seq - 1, 0.0, pltpu.roll(h, seq - 1, axis=0))
    cw = cw_ref[...]
    conv = cw[0:1, :] * h_prev + cw[1:2, :] * h + cw[2:3, :] * h_next + cb_ref[...]
    y = _dot(xb, wb_ref[...]) * conv * _silu(_dot(xb, wz_ref[...]))
    y_scr[cb] = y.astype(BF16)

    @pl.when(cb == n_cb - 1)
    def _():
        ya = _dot(y_scr[0], wpa_ref[0:CONV_CB, :])
        for k in range(1, y_scr.shape[0]):
            ya = ya + _dot(y_scr[k], wpa_ref[k * CONV_CB:(k + 1) * CONV_CB, :])
        ya_ref[...] = ya.astype(BF16)


def _conv_branch(x, w_in, conv_w, conv_b, w_proj_a):
    bsz, seq, d = x.shape
    width = w_proj_a.shape[0]
    n_cb = width // CONV_CB
    blk = lambda g: pl.BlockSpec((d, CONV_CB), lambda b, c, g=g: (0, g * n_cb + c))
    return pl.pallas_call(
        _conv_branch_kernel,
        grid=(bsz, n_cb),
        in_specs=[
            pl.BlockSpec((None, seq, d), lambda b, c: (b, 0, 0)),
            blk(0), blk(1), blk(2), blk(3),
            pl.BlockSpec((CONV_K, CONV_CB), lambda b, c: (0, c)),
            pl.BlockSpec((1, CONV_CB), lambda b, c: (0, c)),
            pl.BlockSpec((width, d), lambda b, c: (0, 0)),
        ],
        out_specs=pl.BlockSpec((None, seq, d), lambda b, c: (b, 0, 0)),
        out_shape=jax.ShapeDtypeStruct((bsz, seq, d), BF16),
        scratch_shapes=[pltpu.VMEM((seq, d), BF16), pltpu.VMEM((n_cb, seq, CONV_CB), BF16)],
        compiler_params=pltpu.CompilerParams(
            dimension_semantics=("arbitrary", "arbitrary"), vmem_limit_bytes=VMEM_LIMIT_BYTES),
        name="conv_branch",
    )(x, w_in, w_in, w_in, w_in, conv_w, conv_b, w_proj_a)


def _qkvzg_kernel(x_ref, w_ref, o_ref, xb_scr):
    j = pl.program_id(1)

    @pl.when(j == 0)
    def _():
        xb_scr[...] = x_ref[...].astype(BF16)

    r = _dot(xb_scr[...], w_ref[...])

    @pl.when(j == 0)
    def _():
        o_ref[...] = (r * (HEAD_DIM ** -0.5 * LOG2E)).astype(BF16)

    @pl.when(jnp.logical_or(j == 1, j == 2))
    def _():
        o_ref[...] = r.astype(BF16)

    @pl.when(j == 3)
    def _():
        o_ref[...] = _silu(r).astype(BF16)

    @pl.when(j >= 4)
    def _():
        o_ref[...] = _sigmoid(r).astype(BF16)


def _qkvzg_proj(x, w_in):
    bsz, seq, d = x.shape
    return pl.pallas_call(
        _qkvzg_kernel,
        grid=(bsz, 6),
        in_specs=[
            pl.BlockSpec((None, seq, d), lambda b, j: (b, 0, 0)),
            pl.BlockSpec((d, d), lambda b, j: (0, 4 + j)),
        ],
        out_specs=pl.BlockSpec((None, None, seq, d), lambda b, j: (j, b, 0, 0)),
        out_shape=jax.ShapeDtypeStruct((6, bsz, seq, d), BF16),
        scratch_shapes=[pltpu.VMEM((seq, d), BF16)],
        compiler_params=pltpu.CompilerParams(
            dimension_semantics=("arbitrary", "arbitrary"), vmem_limit_bytes=VMEM_LIMIT_BYTES),
        name="qkvzg_proj",
    )(x, w_in)


def _diff_attn_kernel(lambda_init, slopes_ref, q_ref, k_ref, v_ref, lq1_ref, lk1_ref, lq2_ref,
                      lk2_ref, g_ref, o_ref, bias_scr, s_scr, p_scr, l_scr):
    head = pl.program_id(0)
    qt = pl.program_id(1)
    b = pl.program_id(2)
    tq = q_ref.shape[0]
    seq = k_ref.shape[0]

    @pl.when(b == 0)
    def _():
        qpos = lax.broadcasted_iota(jnp.int32, (tq, seq), 0) + qt * tq
        kpos = lax.broadcasted_iota(jnp.int32, (tq, seq), 1)
        dist = jnp.abs(qpos - kpos).astype(F32)
        bias_scr[...] = dist * (-LOG2E * slopes_ref[head])

    q = q_ref[...]
    lane = lax.broadcasted_iota(jnp.int32, q.shape, 1)
    zero = jnp.zeros_like(q)
    qs = jnp.concatenate([jnp.where(lane < HEAD_DIM, q, zero),
                          jnp.where(lane >= HEAD_DIM, q, zero)], axis=0)
    s_scr[...] = lax.dot_general(qs, k_ref[...], (((1,), (1,)), ((), ())),
                                 preferred_element_type=F32)

    def softmax_rows(rb, carry):
        r = pl.multiple_of(rb * ATTN_RB, ATTN_RB)
        rq = pl.multiple_of(r - jnp.where(r >= tq, tq, 0), ATTN_RB)
        blk = s_scr[pl.ds(r, ATTN_RB), :] + bias_scr[pl.ds(rq, ATTN_RB), :]
        m = jnp.max(blk, axis=-1, keepdims=True)
        p = jnp.exp2(blk - m)
        l_scr[pl.ds(r, ATTN_RB), :] = jnp.sum(p, axis=-1, keepdims=True)
        p_scr[pl.ds(r, ATTN_RB), :] = p.astype(BF16)
        return carry

    lax.fori_loop(0, 2 * tq // ATTN_RB, softmax_rows, 0)

    o = _dot(p_scr[...], v_ref[...])
    o = o / l_scr[...]
    lam = (jnp.exp(jnp.sum(lq1_ref[...] * lk1_ref[...], keepdims=True))
           - jnp.exp(jnp.sum(lq2_ref[...] * lk2_ref[...], keepdims=True))
           + lambda_init)
    od = o[0:tq, :] - lam * o[tq:2 * tq, :]
    ms = jnp.mean(od * od, axis=-1, keepdims=True)
    on = od * lax.rsqrt(ms + RMS_EPS) * g_ref[...] * (1.0 - lambda_init)
    o_ref[...] = on.astype(BF16)


def _diff_attn(qkvzg, slopes, lq1, lk1, lq2, lk2, subln_g, lambda_init):
    _, bsz, seq, d = qkvzg.shape
    tq = min(ATTN_TQ, seq)
    nq = seq // tq
    vec = lambda n: pl.BlockSpec((1, n), lambda h, t, b: (0, 0))
    return pl.pallas_call(
        functools.partial(_diff_attn_kernel, lambda_init),
        grid=(N_HEADS, nq, bsz),
        in_specs=[
            pl.BlockSpec(memory_space=pltpu.SMEM),
            pl.BlockSpec((None, None, tq, V_DIM), lambda h, t, b: (0, b, t, h)),
            pl.BlockSpec((None, None, seq, V_DIM), lambda h, t, b: (1, b, 0, h)),
            pl.BlockSpec((None, None, seq, V_DIM), lambda h, t, b: (2, b, 0, h)),
            vec(HEAD_DIM), vec(HEAD_DIM), vec(HEAD_DIM), vec(HEAD_DIM), vec(V_DIM),
        ],
        out_specs=pl.BlockSpec((None, tq, V_DIM), lambda h, t, b: (b, t, h)),
        out_shape=jax.ShapeDtypeStruct((bsz, seq, d), BF16),
        scratch_shapes=[
            pltpu.VMEM((tq, seq), F32),
            pltpu.VMEM((2 * tq, seq), F32),
            pltpu.VMEM((2 * tq, seq), BF16),
            pltpu.VMEM((2 * tq, 1), F32),
        ],
        compiler_params=pltpu.CompilerParams(
            dimension_semantics=("arbitrary", "arbitrary", "arbitrary"),
            vmem_limit_bytes=VMEM_LIMIT_BYTES),
        name="diff_attn",
    )(slopes, qkvzg, qkvzg, qkvzg, lq1, lk1, lq2, lk2, subln_g)


def _epilogue_kernel(alpha, on_ref, zs_ref, ga_ref, gb_ref, ya_ref, x_ref, p_ref, wpb_ref,
                     wout_ref, wgate_ref, wple_ref, lng_ref, lnb_ref, out_ref):
    ob = on_ref[...] * zs_ref[...]
    yb = _dot(ob, wpb_ref[...])
    merged = (ga_ref[...].astype(F32) * ya_ref[...].astype(F32)
              + gb_ref[...].astype(F32) * yb)
    r = alpha * x_ref[...] + _dot(merged.astype(BF16), wout_ref[...])
    gate = _sigmoid(_dot(r.astype(BF16), wgate_ref[...]))
    r = r + gate * _dot(p_ref[...].astype(BF16), wple_ref[...])
    mu = jnp.mean(r, axis=-1, keepdims=True)
    rc = r - mu
    var = jnp.mean(rc * rc, axis=-1, keepdims=True)
    out_ref[...] = rc * lax.rsqrt(var + LN_EPS) * lng_ref[...] + lnb_ref[...]


def _epilogue(on, qkvzg, ya, x, p, w_proj_b, w_out, w_gate, w_ple, ln_g, ln_b, alpha):
    n, d = x.shape
    ple = p.shape[1]
    t = min(EPI_T, n)
    nt = n // t
    row = lambda w: pl.BlockSpec((t, w), lambda i: (i, 0))
    full = lambda r, c: pl.BlockSpec((r, c), lambda i: (0, 0))
    return pl.pallas_call(
        functools.partial(_epilogue_kernel, alpha),
        grid=(nt,),
        in_specs=[
            row(d),
            pl.BlockSpec((None, t, d), lambda i: (3, i, 0)),
            pl.BlockSpec((None, t, d), lambda i: (4, i, 0)),
            pl.BlockSpec((None, t, d), lambda i: (5, i, 0)),
            row(d), row(d), row(ple),
            full(d, d), full(d, d), full(d, d), full(ple, d), full(1, d), full(1, d),
        ],
        out_specs=row(d),
        out_shape=jax.ShapeDtypeStruct((n, d), F32),
        compiler_params=pltpu.CompilerParams(
            dimension_semantics=("arbitrary",), vmem_limit_bytes=VMEM_LIMIT_BYTES),
        name="epilogue",
    )(on, qkvzg, qkvzg, qkvzg, ya, x, p, w_proj_b, w_out, w_gate, w_ple, ln_g, ln_b)


def kernel(x, p, w_in, conv_w, conv_b, w_proj_a, lambda_q1, lambda_k1, lambda_q2, lambda_k2,
           subln_g, w_proj_b, w_out, w_ple, w_ple_gate, ln_g, ln_b):
    depth = w_in.shape[0]
    bsz, seq, d = x.shape
    alpha = (2.0 * depth) ** 0.25
    slopes = jnp.asarray(
        np.float32(2.0) ** (-8.0 * np.arange(1, N_HEADS + 1, dtype=np.float32) / N_HEADS))
    h = x
    for i in range(depth):
        lambda_init = 0.8 - 0.6 * math.exp(-0.3 * i)
        w_in_b = w_in[i].astype(BF16)
        ya = _conv_branch(h, w_in_b, conv_w[i], conv_b[i][None, :], w_proj_a[i].astype(BF16))
        qkvzg = _qkvzg_proj(h, w_in_b)
        on = _diff_attn(qkvzg, slopes, lambda_q1[i][None, :], lambda_k1[i][None, :],
                        lambda_q2[i][None, :], lambda_k2[i][None, :], subln_g[i][None, :],
                        lambda_init)
        out = _epilogue(on.reshape(bsz * seq, d), qkvzg.reshape(6, bsz * seq, d),
                        ya.reshape(bsz * seq, d), h.reshape(bsz * seq, d),
                        p[i].reshape(bsz * seq, -1), w_proj_b[i].astype(BF16),
                        w_out[i].astype(BF16), w_ple_gate[i].astype(BF16),
                        w_ple[i].astype(BF16), ln_g[i][None, :], ln_b[i][None, :], alpha)
        h = out.reshape(bsz, seq, d)
    return h
```

```python
import functools
import math

import jax
import jax.numpy as jnp
import numpy as np
from jax import lax
from jax.experimental import pallas as pl
from jax.experimental.pallas import tpu as pltpu

N_HEADS = 8
HEAD_DIM = 64
V_DIM = 2 * HEAD_DIM
LN_EPS = 1e-5
RMS_EPS = 1e-5
CONV_K = 3
LOG2E = math.log2(math.e)

VMEM_LIMIT_BYTES = 56 * 1024 * 1024

CONV_CB = 256
ATTN_TQ = 256
ATTN_KC = 128
EPI_T = 512

BF16 = jnp.bfloat16
F32 = jnp.float32


def _dot(a, b):
    return jnp.dot(a, b, preferred_element_type=F32)


def _sigmoid(x):
    return 1.0 / (1.0 + jnp.exp(-x))


def _silu(x):
    return x * _sigmoid(x)


def _conv_branch_kernel(x_ref, wu_ref, wc_ref, wb_ref, wz_ref, cw_ref, cb_ref,
                        wpa_ref, ya_ref, xb_scr, y_scr):
    cb = pl.program_id(1)
    n_cb = pl.num_programs(1)
    seq = x_ref.shape[0]

    @pl.when(cb == 0)
    def _():
        xb_scr[...] = x_ref[...].astype(BF16)

    xb = xb_scr[...]
    h = _dot(xb, wc_ref[...]) * _dot(xb, wu_ref[...])
    row = lax.broadcasted_iota(jnp.int32, h.shape, 0)
    h_prev = jnp.where(row == 0, 0.0, pltpu.roll(h, 1, axis=0))
    h_next = jnp.where(row == seq - 1, 0.0, pltpu.roll(h, seq - 1, axis=0))
    cw = cw_ref[...]
    conv = cw[0:1, :] * h_prev + cw[1:2, :] * h + cw[2:3, :] * h_next + cb_ref[...]
    y = _dot(xb, wb_ref[...]) * conv * _silu(_dot(xb, wz_ref[...]))
    y_scr[cb] = y.astype(BF16)

    @pl.when(cb == n_cb - 1)
    def _():
        ya = _dot(y_scr[0], wpa_ref[0:CONV_CB, :])
        for k in range(1, y_scr.shape[0]):
            ya = ya + _dot(y_scr[k], wpa_ref[k * CONV_CB:(k + 1) * CONV_CB, :])
        ya_ref[...] = ya.astype(BF16)


def _conv_branch(x, w_in, conv_w, conv_b, w_proj_a):
    bsz, seq, d = x.shape
    width = w_proj_a.shape[0]
    n_cb = width // CONV_CB
    blk = lambda g: pl.BlockSpec((d, CONV_CB), lambda b, c, g=g: (0, g * n_cb + c))
    return pl.pallas_call(
        _conv_branch_kernel,
        grid=(bsz, n_cb),
        in_specs=[
            pl.BlockSpec((None, seq, d), lambda b, c: (b, 0, 0)),
            blk(0), blk(1), blk(2), blk(3),
            pl.BlockSpec((CONV_K, CONV_CB), lambda b, c: (0, c)),
            pl.BlockSpec((1, CONV_CB), lambda b, c: (0, c)),
            pl.BlockSpec((width, d), lambda b, c: (0, 0)),
        ],
        out_specs=pl.BlockSpec((None, seq, d), lambda b, c: (b, 0, 0)),
        out_shape=jax.ShapeDtypeStruct((bsz, seq, d), BF16),
        scratch_shapes=[pltpu.VMEM((seq, d), BF16), pltpu.VMEM((n_cb, seq, CONV_CB), BF16)],
        compiler_params=pltpu.CompilerParams(
            dimension_semantics=("arbitrary", "arbitrary"), vmem_limit_bytes=VMEM_LIMIT_BYTES),
        name="conv_branch",
    )(x, w_in, w_in, w_in, w_in, conv_w, conv_b, w_proj_a)


N_PROJ = 7


def _proj_kernel(x_ref, wq_ref, w_ref, o_ref, xb_scr):
    j = pl.program_id(1)

    @pl.when(j == 0)
    def _():
        xb_scr[...] = x_ref[...].astype(BF16)

    @pl.when(j < 2)
    def _():
        r = _dot(xb_scr[...], wq_ref[...]) * (HEAD_DIM ** -0.5 * LOG2E)
        lane = lax.broadcasted_iota(jnp.int32, (r.shape[0], V_DIM), 1)
        parts = []
        for h in range(r.shape[1] // V_DIM):
            rh = r[:, h * V_DIM:(h + 1) * V_DIM]
            parts.append(jnp.where(lane < HEAD_DIM, rh, 0.0))
            parts.append(jnp.where(lane >= HEAD_DIM, rh, 0.0))
        o_ref[...] = jnp.concatenate(parts, axis=1).astype(BF16)

    @pl.when(jnp.logical_or(j == 2, j == 3))
    def _():
        o_ref[...] = _dot(xb_scr[...], w_ref[...]).astype(BF16)

    @pl.when(j == 4)
    def _():
        o_ref[...] = _silu(_dot(xb_scr[...], w_ref[...])).astype(BF16)

    @pl.when(j >= 5)
    def _():
        o_ref[...] = _sigmoid(_dot(xb_scr[...], w_ref[...])).astype(BF16)


def _proj(x, w_in):
    bsz, seq, d = x.shape
    return pl.pallas_call(
        _proj_kernel,
        grid=(bsz, N_PROJ),
        in_specs=[
            pl.BlockSpec((None, seq, d), lambda b, j: (b, 0, 0)),
            pl.BlockSpec((d, d // 2), lambda b, j: (0, 8 + jnp.minimum(j, 1))),
            pl.BlockSpec((d, d), lambda b, j: (0, 3 + jnp.maximum(j, 2))),
        ],
        out_specs=pl.BlockSpec((None, None, seq, d), lambda b, j: (j, b, 0, 0)),
        out_shape=jax.ShapeDtypeStruct((N_PROJ, bsz, seq, d), BF16),
        scratch_shapes=[pltpu.VMEM((seq, d), BF16)],
        compiler_params=pltpu.CompilerParams(
            dimension_semantics=("arbitrary", "arbitrary"), vmem_limit_bytes=VMEM_LIMIT_BYTES),
        name="proj",
    )(x, w_in, w_in)


def _diff_attn_kernel(lambda_init, slopes_ref, q_ref, k_ref, v_ref, lq1_ref, lk1_ref, lq2_ref,
                      lk2_ref, g_ref, o_ref, bias_scr, s_scr, p_scr):
    head = pl.program_id(0)
    b = pl.program_id(1)
    qt = pl.program_id(2)
    tq = q_ref.shape[0]
    seq = k_ref.shape[0]

    @pl.when(jnp.logical_and(b == 0, qt == 0))
    def _():
        r = lax.broadcasted_iota(jnp.int32, bias_scr.shape, 0)
        i = lax.broadcasted_iota(jnp.int32, bias_scr.shape, 1)
        dist = jnp.abs(r - (seq - tq) - i).astype(F32)
        bias_scr[...] = dist * (-LOG2E * slopes_ref[head])

    q = q_ref[...]
    qs = jnp.concatenate([q[:, 0:V_DIM], q[:, V_DIM:2 * V_DIM]], axis=0)
    s = lax.dot_general(k_ref[...], qs, (((1,), (1,)), ((), ())),
                        preferred_element_type=F32)
    start = pl.multiple_of((seq - tq) - qt * tq, tq)
    bias = bias_scr[pl.ds(start, seq), :]
    s = s + jnp.concatenate([bias, bias], axis=1)
    m = jnp.max(s, axis=0, keepdims=True)
    s_scr[...] = s

    l = jnp.zeros((1, 2 * tq), F32)
    for c in range(seq // ATTN_KC):
        rows = slice(c * ATTN_KC, (c + 1) * ATTN_KC)
        pc = jnp.exp2(s_scr[rows, :] - m)
        l = l + jnp.sum(pc, axis=0, keepdims=True)
        p_scr[rows, :] = pc.astype(BF16)

    ot = lax.dot_general(v_ref[...], p_scr[...], (((0,), (0,)), ((), ())),
                         preferred_element_type=F32)
    ot = ot / l
    lam = (jnp.exp(jnp.sum(lq1_ref[...] * lk1_ref[...], keepdims=True))
           - jnp.exp(jnp.sum(lq2_ref[...] * lk2_ref[...], keepdims=True))
           + lambda_init)
    odt = ot[:, 0:tq] - lam * ot[:, tq:2 * tq]
    ms = jnp.mean(odt * odt, axis=0, keepdims=True)
    ont = odt * lax.rsqrt(ms + RMS_EPS) * g_ref[...] * (1.0 - lambda_init)
    o_ref[...] = ont.T.astype(BF16)


def _diff_attn(proj, slopes, lq1, lk1, lq2, lk2, subln_g, lambda_init):
    _, bsz, seq, d = proj.shape
    tq = min(ATTN_TQ, seq)
    nq = seq // tq
    hpg = d // (2 * V_DIM)
    vec = lambda n: pl.BlockSpec((1, n), lambda h, b, t: (0, 0))
    return pl.pallas_call(
        functools.partial(_diff_attn_kernel, lambda_init),
        grid=(N_HEADS, bsz, nq),
        in_specs=[
            pl.BlockSpec(memory_space=pltpu.SMEM),
            pl.BlockSpec((None, None, tq, 2 * V_DIM), lambda h, b, t: (h // hpg, b, t, h % hpg)),
            pl.BlockSpec((None, None, seq, V_DIM), lambda h, b, t: (2, b, 0, h)),
            pl.BlockSpec((None, None, seq, V_DIM), lambda h, b, t: (3, b, 0, h)),
            vec(HEAD_DIM), vec(HEAD_DIM), vec(HEAD_DIM), vec(HEAD_DIM),
            pl.BlockSpec((V_DIM, 1), lambda h, b, t: (0, 0)),
        ],
        out_specs=pl.BlockSpec((None, tq, V_DIM), lambda h, b, t: (b, t, h)),
        out_shape=jax.ShapeDtypeStruct((bsz, seq, d), BF16),
        scratch_shapes=[
            pltpu.VMEM((2 * seq - tq, tq), F32),
            pltpu.VMEM((seq, 2 * tq), F32),
            pltpu.VMEM((seq, 2 * tq), BF16),
        ],
        compiler_params=pltpu.CompilerParams(
            dimension_semantics=("arbitrary", "arbitrary", "arbitrary"),
            vmem_limit_bytes=VMEM_LIMIT_BYTES),
        name="diff_attn",
    )(slopes, proj, proj, proj, lq1, lk1, lq2, lk2, subln_g)


def _epilogue_kernel(alpha, on_ref, zs_ref, ga_ref, gb_ref, ya_ref, x_ref, p_ref, wpb_ref,
                     wout_ref, wgate_ref, wple_ref, lng_ref, lnb_ref, out_ref):
    ob = on_ref[...] * zs_ref[...]
    yb = _dot(ob, wpb_ref[...])
    merged = (ga_ref[...].astype(F32) * ya_ref[...].astype(F32)
              + gb_ref[...].astype(F32) * yb)
    r = alpha * x_ref[...] + _dot(merged.astype(BF16), wout_ref[...])
    gate = _sigmoid(_dot(r.astype(BF16), wgate_ref[...]))
    r = r + gate * _dot(p_ref[...].astype(BF16), wple_ref[...])
    mu = jnp.mean(r, axis=-1, keepdims=True)
    rc = r - mu
    var = jnp.mean(rc * rc, axis=-1, keepdims=True)
    out_ref[...] = rc * lax.rsqrt(var + LN_EPS) * lng_ref[...] + lnb_ref[...]


def _epilogue(on, proj, ya, x, p, w_proj_b, w_out, w_gate, w_ple, ln_g, ln_b, alpha):
    n, d = x.shape
    ple = p.shape[1]
    t = min(EPI_T, n)
    nt = n // t
    row = lambda w: pl.BlockSpec((t, w), lambda i: (i, 0))
    full = lambda r, c: pl.BlockSpec((r, c), lambda i: (0, 0))
    return pl.pallas_call(
        functools.partial(_epilogue_kernel, alpha),
        grid=(nt,),
        in_specs=[
            row(d),
            pl.BlockSpec((None, t, d), lambda i: (4, i, 0)),
            pl.BlockSpec((None, t, d), lambda i: (5, i, 0)),
            pl.BlockSpec((None, t, d), lambda i: (6, i, 0)),
            row(d), row(d), row(ple),
            full(d, d), full(d, d), full(d, d), full(ple, d), full(1, d), full(1, d),
        ],
        out_specs=row(d),
        out_shape=jax.ShapeDtypeStruct((n, d), F32),
        compiler_params=pltpu.CompilerParams(
            dimension_semantics=("arbitrary",), vmem_limit_bytes=VMEM_LIMIT_BYTES),
        name="epilogue",
    )(on, proj, proj, proj, ya, x, p, w_proj_b, w_out, w_gate, w_ple, ln_g, ln_b)


def kernel(x, p, w_in, conv_w, conv_b, w_proj_a, lambda_q1, lambda_k1, lambda_q2, lambda_k2,
           subln_g, w_proj_b, w_out, w_ple, w_ple_gate, ln_g, ln_b):
    depth = w_in.shape[0]
    bsz, seq, d = x.shape
    alpha = (2.0 * depth) ** 0.25
    slopes = jnp.asarray(
        np.float32(2.0) ** (-8.0 * np.arange(1, N_HEADS + 1, dtype=np.float32) / N_HEADS))
    h = x
    for i in range(depth):
        lambda_init = 0.8 - 0.6 * math.exp(-0.3 * i)
        w_in_b = w_in[i].astype(BF16)
        ya = _conv_branch(h, w_in_b, conv_w[i], conv_b[i][None, :], w_proj_a[i].astype(BF16))
        proj = _proj(h, w_in_b)
        on = _diff_attn(proj, slopes, lambda_q1[i][None, :], lambda_k1[i][None, :],
                        lambda_q2[i][None, :], lambda_k2[i][None, :], subln_g[i][:, None],
                        lambda_init)
        out = _epilogue(on.reshape(bsz * seq, d), proj.reshape(N_PROJ, bsz * seq, d),
                        ya.reshape(bsz * seq, d), h.reshape(bsz * seq, d),
                        p[i].reshape(bsz * seq, -1), w_proj_b[i].astype(BF16),
                        w_out[i].astype(BF16), w_ple_gate[i].astype(BF16),
                        w_ple[i].astype(BF16), ln_g[i][None, :], ln_b[i][None, :], alpha)
        h = out.reshape(bsz, seq, d)
    return h
```

```python
import functools
import math

import jax
import jax.numpy as jnp
import numpy as np
from jax import lax
from jax.experimental import pallas as pl
from jax.experimental.pallas import tpu as pltpu

N_HEADS = 8
HEAD_DIM = 64
V_DIM = 2 * HEAD_DIM
LN_EPS = 1e-5
RMS_EPS = 1e-5
CONV_K = 3
LOG2E = math.log2(math.e)

VMEM_LIMIT_BYTES = 56 * 1024 * 1024

CONV_CB = 256
ATTN_TQ = 256
ATTN_KC = 256
EPI_T = 512

BF16 = jnp.bfloat16
F32 = jnp.float32


def _dot(a, b):
    return jnp.dot(a, b, preferred_element_type=F32)


def _sigmoid(x):
    return 1.0 / (1.0 + jnp.exp(-x))


def _silu(x):
    return x * _sigmoid(x)


def _conv_branch_kernel(x_ref, wu_ref, wc_ref, wb_ref, wz_ref, cw_ref, cb_ref,
                        wpa_ref, ya_ref, xb_scr, y_scr):
    cb = pl.program_id(1)
    n_cb = pl.num_programs(1)
    seq = x_ref.shape[0]

    @pl.when(cb == 0)
    def _():
        xb_scr[...] = x_ref[...].astype(BF16)

    xb = xb_scr[...]
    h = _dot(xb, wc_ref[...]) * _dot(xb, wu_ref[...])
    row = lax.broadcasted_iota(jnp.int32, h.shape, 0)
    h_prev = jnp.where(row == 0, 0.0, pltpu.roll(h, 1, axis=0))
    h_next = jnp.where(row == seq - 1, 0.0, pltpu.roll(h, seq - 1, axis=0))
    cw = cw_ref[...]
    conv = cw[0:1, :] * h_prev + cw[1:2, :] * h + cw[2:3, :] * h_next + cb_ref[...]
    y = _dot(xb, wb_ref[...]) * conv * _silu(_dot(xb, wz_ref[...]))
    y_scr[cb] = y.astype(BF16)

    @pl.when(cb == n_cb - 1)
    def _():
        ya = _dot(y_scr[0], wpa_ref[0:CONV_CB, :])
        for k in range(1, y_scr.shape[0]):
            ya = ya + _dot(y_scr[k], wpa_ref[k * CONV_CB:(k + 1) * CONV_CB, :])
        ya_ref[...] = ya.astype(BF16)


def _conv_branch(x, w_in, conv_w, conv_b, w_proj_a):
    bsz, seq, d = x.shape
    width = w_proj_a.shape[0]
    n_cb = width // CONV_CB
    blk = lambda g: pl.BlockSpec((d, CONV_CB), lambda b, c, g=g: (0, g * n_cb + c))
    return pl.pallas_call(
        _conv_branch_kernel,
        grid=(bsz, n_cb),
        in_specs=[
            pl.BlockSpec((None, seq, d), lambda b, c: (b, 0, 0)),
            blk(0), blk(1), blk(2), blk(3),
            pl.BlockSpec((CONV_K, CONV_CB), lambda b, c: (0, c)),
            pl.BlockSpec((1, CONV_CB), lambda b, c: (0, c)),
            pl.BlockSpec((width, d), lambda b, c: (0, 0)),
        ],
        out_specs=pl.BlockSpec((None, seq, d), lambda b, c: (b, 0, 0)),
        out_shape=jax.ShapeDtypeStruct((bsz, seq, d), BF16),
        scratch_shapes=[pltpu.VMEM((seq, d), BF16), pltpu.VMEM((n_cb, seq, CONV_CB), BF16)],
        compiler_params=pltpu.CompilerParams(
            dimension_semantics=("arbitrary", "arbitrary"), vmem_limit_bytes=VMEM_LIMIT_BYTES),
        name="conv_branch",
    )(x, w_in, w_in, w_in, w_in, conv_w, conv_b, w_proj_a)


N_PROJ = 7


def _proj_kernel(x_ref, wq_ref, w_ref, o_ref, xb_scr):
    j = pl.program_id(1)

    @pl.when(j == 0)
    def _():
        xb_scr[...] = x_ref[...].astype(BF16)

    @pl.when(j < 2)
    def _():
        r = _dot(xb_scr[...], wq_ref[...]) * (HEAD_DIM ** -0.5 * LOG2E)
        lane = lax.broadcasted_iota(jnp.int32, (r.shape[0], V_DIM), 1)
        parts = []
        for h in range(r.shape[1] // V_DIM):
            rh = r[:, h * V_DIM:(h + 1) * V_DIM]
            parts.append(jnp.where(lane < HEAD_DIM, rh, 0.0))
            parts.append(jnp.where(lane >= HEAD_DIM, rh, 0.0))
        o_ref[...] = jnp.concatenate(parts, axis=1).astype(BF16)

    @pl.when(jnp.logical_or(j == 2, j == 3))
    def _():
        o_ref[...] = _dot(xb_scr[...], w_ref[...]).astype(BF16)

    @pl.when(j == 4)
    def _():
        o_ref[...] = _silu(_dot(xb_scr[...], w_ref[...])).astype(BF16)

    @pl.when(j >= 5)
    def _():
        o_ref[...] = _sigmoid(_dot(xb_scr[...], w_ref[...])).astype(BF16)


def _proj(x, w_in):
    bsz, seq, d = x.shape
    return pl.pallas_call(
        _proj_kernel,
        grid=(bsz, N_PROJ),
        in_specs=[
            pl.BlockSpec((None, seq, d), lambda b, j: (b, 0, 0)),
            pl.BlockSpec((d, d // 2), lambda b, j: (0, 8 + jnp.minimum(j, 1))),
            pl.BlockSpec((d, d), lambda b, j: (0, 3 + jnp.maximum(j, 2))),
        ],
        out_specs=pl.BlockSpec((None, None, seq, d), lambda b, j: (j, b, 0, 0)),
        out_shape=jax.ShapeDtypeStruct((N_PROJ, bsz, seq, d), BF16),
        scratch_shapes=[pltpu.VMEM((seq, d), BF16)],
        compiler_params=pltpu.CompilerParams(
            dimension_semantics=("arbitrary", "arbitrary"), vmem_limit_bytes=VMEM_LIMIT_BYTES),
        name="proj",
    )(x, w_in, w_in)


def _diff_attn_kernel(lambda_init, tq, slopes_ref, q_ref, k_ref, v_ref, lq1_ref, lk1_ref,
                      lq2_ref, lk2_ref, g_ref, o_ref, bias_scr, qs_scr, s0_scr, s1_scr, p0_scr,
                      p1_scr):
    s_scr = (s0_scr, s1_scr)
    p_scr = (p0_scr, p1_scr)
    head = pl.program_id(0)
    b = pl.program_id(1)
    seq = k_ref.shape[0]
    nq = seq // tq
    kc = min(ATTN_KC, seq)
    n_kc = seq // kc
    sub = 8

    @pl.when(b == 0)
    def _():
        r = lax.broadcasted_iota(jnp.int32, bias_scr.shape, 0)
        i = lax.broadcasted_iota(jnp.int32, bias_scr.shape, 1)
        dist = jnp.abs(r - (seq - tq) - i).astype(F32)
        bias_scr[...] = dist * (-LOG2E * slopes_ref[head])

    lam = (jnp.exp(jnp.sum(lq1_ref[...] * lk1_ref[...], keepdims=True))
           - jnp.exp(jnp.sum(lq2_ref[...] * lk2_ref[...], keepdims=True))
           + lambda_init)

    def stage_queries(t):
        q = q_ref[t * tq:(t + 1) * tq, :]
        qs_scr[...] = jnp.concatenate([q[:, 0:V_DIM], q[:, V_DIM:2 * V_DIM]], axis=0)

    def phase_a_chunk(t, i, m8):
        r0 = i * kc
        s = lax.dot_general(k_ref[r0:r0 + kc, :], qs_scr[...], (((1,), (1,)), ((), ())),
                            preferred_element_type=F32)
        b0 = (seq - tq) - t * tq + r0
        bias = bias_scr[b0:b0 + kc, :]
        s = s + jnp.concatenate([bias, bias], axis=1)
        s_scr[t % 2][r0:r0 + kc, :] = s
        return jnp.maximum(m8, jnp.max(s.reshape(kc // sub, sub, 2 * tq), axis=0))

    def phase_b_chunk(t, i, m8, l8):
        r0 = i * kc
        s = s_scr[t % 2][r0:r0 + kc, :].reshape(kc // sub, sub, 2 * tq)
        p = jnp.exp2(s - m8[None])
        p_scr[t % 2][r0:r0 + kc, :] = p.reshape(kc, 2 * tq).astype(BF16)
        return l8 + jnp.sum(p, axis=0)

    def column_max(m8):
        return jnp.broadcast_to(jnp.max(m8, axis=0, keepdims=True), m8.shape)

    def finalize(t, l8):
        ot = lax.dot_general(v_ref[...], p_scr[t % 2][...], (((0,), (0,)), ((), ())),
                             preferred_element_type=F32)
        ot = ot / jnp.sum(l8, axis=0, keepdims=True)
        odt = ot[:, 0:tq] - lam * ot[:, tq:2 * tq]
        ms = jnp.mean(odt * odt, axis=0, keepdims=True)
        ont = odt * lax.rsqrt(ms + RMS_EPS) * g_ref[...] * (1.0 - lambda_init)
        o_ref[t * tq:(t + 1) * tq, :] = ont.T.astype(BF16)

    neg_inf = jnp.full((sub, 2 * tq), -jnp.inf, F32)
    zeros = jnp.zeros((sub, 2 * tq), F32)

    m8 = {}
    l8 = {}
    for t in range(-1, nq + 1):
        if 0 <= t - 1 < nq:
            finalize(t - 1, l8.pop(t - 1))
        if t + 1 < nq:
            stage_queries(t + 1)
            m_acc = neg_inf
        if 0 <= t < nq:
            l_acc = zeros
        for i in range(n_kc):
            if t + 1 < nq:
                m_acc = phase_a_chunk(t + 1, i, m_acc)
            if 0 <= t < nq:
                l_acc = phase_b_chunk(t, i, m8[t], l_acc)
        if t + 1 < nq:
            m8[t + 1] = column_max(m_acc)
        if 0 <= t < nq:
            l8[t] = l_acc
            del m8[t]


def _diff_attn(proj, slopes, lq1, lk1, lq2, lk2, subln_g, lambda_init):
    _, bsz, seq, d = proj.shape
    tq = min(ATTN_TQ, seq)
    hpg = d // (2 * V_DIM)
    vec = lambda n: pl.BlockSpec((1, n), lambda h, b: (0, 0))
    return pl.pallas_call(
        functools.partial(_diff_attn_kernel, lambda_init, tq),
        grid=(N_HEADS, bsz),
        in_specs=[
            pl.BlockSpec(memory_space=pltpu.SMEM),
            pl.BlockSpec((None, None, seq, 2 * V_DIM), lambda h, b: (h // hpg, b, 0, h % hpg)),
            pl.BlockSpec((None, None, seq, V_DIM), lambda h, b: (2, b, 0, h)),
            pl.BlockSpec((None, None, seq, V_DIM), lambda h, b: (3, b, 0, h)),
            vec(HEAD_DIM), vec(HEAD_DIM), vec(HEAD_DIM), vec(HEAD_DIM),
            pl.BlockSpec((V_DIM, 1), lambda h, b: (0, 0)),
        ],
        out_specs=pl.BlockSpec((None, seq, V_DIM), lambda h, b: (b, 0, h)),
        out_shape=jax.ShapeDtypeStruct((bsz, seq, d), BF16),
        scratch_shapes=[
            pltpu.VMEM((2 * seq - tq, tq), F32),
            pltpu.VMEM((2 * tq, V_DIM), BF16),
            pltpu.VMEM((seq, 2 * tq), F32),
            pltpu.VMEM((seq, 2 * tq), F32),
            pltpu.VMEM((seq, 2 * tq), BF16),
            pltpu.VMEM((seq, 2 * tq), BF16),
        ],
        compiler_params=pltpu.CompilerParams(
            dimension_semantics=("arbitrary", "arbitrary"),
            vmem_limit_bytes=VMEM_LIMIT_BYTES),
        name="diff_attn",
    )(slopes, proj, proj, proj, lq1, lk1, lq2, lk2, subln_g)


def _epilogue_kernel(alpha, on_ref, zs_ref, ga_ref, gb_ref, ya_ref, x_ref, p_ref, wpb_ref,
                     wout_ref, wgate_ref, wple_ref, lng_ref, lnb_ref, out_ref):
    ob = on_ref[...] * zs_ref[...]
    yb = _dot(ob, wpb_ref[...])
    merged = (ga_ref[...].astype(F32) * ya_ref[...].astype(F32)
              + gb_ref[...].astype(F32) * yb)
    r = alpha * x_ref[...] + _dot(merged.astype(BF16), wout_ref[...])
    gate = _sigmoid(_dot(r.astype(BF16), wgate_ref[...]))
    r = r + gate * _dot(p_ref[...].astype(BF16), wple_ref[...])
    mu = jnp.mean(r, axis=-1, keepdims=True)
    rc = r - mu
    var = jnp.mean(rc * rc, axis=-1, keepdims=True)
    out_ref[...] = rc * lax.rsqrt(var + LN_EPS) * lng_ref[...] + lnb_ref[...]


def _epilogue(on, proj, ya, x, p, w_proj_b, w_out, w_gate, w_ple, ln_g, ln_b, alpha):
    n, d = x.shape
    ple = p.shape[1]
    t = min(EPI_T, n)
    nt = n // t
    row = lambda w: pl.BlockSpec((t, w), lambda i: (i, 0))
    full = lambda r, c: pl.BlockSpec((r, c), lambda i: (0, 0))
    return pl.pallas_call(
        functools.partial(_epilogue_kernel, alpha),
        grid=(nt,),
        in_specs=[
            row(d),
            pl.BlockSpec((None, t, d), lambda i: (4, i, 0)),
            pl.BlockSpec((None, t, d), lambda i: (5, i, 0)),
            pl.BlockSpec((None, t, d), lambda i: (6, i, 0)),
            row(d), row(d), row(ple),
            full(d, d), full(d, d), full(d, d), full(ple, d), full(1, d), full(1, d),
        ],
        out_specs=row(d),
        out_shape=jax.ShapeDtypeStruct((n, d), F32),
        compiler_params=pltpu.CompilerParams(
            dimension_semantics=("arbitrary",), vmem_limit_bytes=VMEM_LIMIT_BYTES),
        name="epilogue",
    )(on, proj, proj, proj, ya, x, p, w_proj_b, w_out, w_gate, w_ple, ln_g, ln_b)


def kernel(x, p, w_in, conv_w, conv_b, w_proj_a, lambda_q1, lambda_k1, lambda_q2, lambda_k2,
           subln_g, w_proj_b, w_out, w_ple, w_ple_gate, ln_g, ln_b):
    depth = w_in.shape[0]
    bsz, seq, d = x.shape
    alpha = (2.0 * depth) ** 0.25
    slopes = jnp.asarray(
        np.float32(2.0) ** (-8.0 * np.arange(1, N_HEADS + 1, dtype=np.float32) / N_HEADS))
    h = x
    for i in range(depth):
        lambda_init = 0.8 - 0.6 * math.exp(-0.3 * i)
        w_in_b = w_in[i].astype(BF16)
        ya = _conv_branch(h, w_in_b, conv_w[i], conv_b[i][None, :], w_proj_a[i].astype(BF16))
        proj = _proj(h, w_in_b)
        on = _diff_attn(proj, slopes, lambda_q1[i][None, :], lambda_k1[i][None, :],
                        lambda_q2[i][None, :], lambda_k2[i][None, :], subln_g[i][:, None],
                        lambda_init)
        out = _epilogue(on.reshape(bsz * seq, d), proj.reshape(N_PROJ, bsz * seq, d),
                        ya.reshape(bsz * seq, d), h.reshape(bsz * seq, d),
                        p[i].reshape(bsz * seq, -1), w_proj_b[i].astype(BF16),
                        w_out[i].astype(BF16), w_ple_gate[i].astype(BF16),
                        w_ple[i].astype(BF16), ln_g[i][None, :], ln_b[i][None, :], alpha)
        h = out.reshape(bsz, seq, d)
    return h
```

```python
import functools
import math

import jax
import jax.numpy as jnp
import numpy as np
from jax import lax
from jax.experimental import pallas as pl
from jax.experimental.pallas import tpu as pltpu

N_HEADS = 8
HEAD_DIM = 64
V_DIM = 2 * HEAD_DIM
LN_EPS = 1e-5
RMS_EPS = 1e-5
CONV_K = 3
LOG2E = math.log2(math.e)

VMEM_LIMIT_BYTES = 56 * 1024 * 1024

CONV_CB = 256
ATTN_TQ = 256
ATTN_KC = 256
EPI_T = 1024
EPI_SPLIT = 4

BF16 = jnp.bfloat16
F32 = jnp.float32


def _dot(a, b):
    return jnp.dot(a, b, preferred_element_type=F32)


def _sigmoid(x):
    return 0.5 * jnp.tanh(0.5 * x) + 0.5


def _silu(x):
    return x * _sigmoid(x)


def _conv_branch_kernel(x_ref, wu_ref, wc_ref, wb_ref, wz_ref, cw_ref, cb_ref,
                        wpa_ref, ya_ref, xb_scr, y_scr):
    cb = pl.program_id(1)
    n_cb = pl.num_programs(1)
    seq = x_ref.shape[0]

    @pl.when(cb == 0)
    def _():
        xb_scr[...] = x_ref[...].astype(BF16)

    xb = xb_scr[...]
    h = _dot(xb, wc_ref[...]) * _dot(xb, wu_ref[...])
    row = lax.broadcasted_iota(jnp.int32, h.shape, 0)
    h_prev = jnp.where(row == 0, 0.0, pltpu.roll(h, 1, axis=0))
    h_next = jnp.where(row == seq - 1, 0.0, pltpu.roll(h, seq - 1, axis=0))
    cw = cw_ref[...]
    conv = cw[0:1, :] * h_prev + cw[1:2, :] * h + cw[2:3, :] * h_next + cb_ref[...]
    y = _dot(xb, wb_ref[...]) * conv * _silu(_dot(xb, wz_ref[...]))
    y_scr[cb] = y.astype(BF16)

    @pl.when(cb == n_cb - 1)
    def _():
        ya = _dot(y_scr[0], wpa_ref[0:CONV_CB, :])
        for k in range(1, y_scr.shape[0]):
            ya = ya + _dot(y_scr[k], wpa_ref[k * CONV_CB:(k + 1) * CONV_CB, :])
        ya_ref[...] = ya.astype(BF16)


def _conv_branch(x, w_in, conv_w, conv_b, w_proj_a):
    bsz, seq, d = x.shape
    width = w_proj_a.shape[0]
    n_cb = width // CONV_CB
    blk = lambda g: pl.BlockSpec((d, CONV_CB), lambda b, c, g=g: (0, g * n_cb + c))
    return pl.pallas_call(
        _conv_branch_kernel,
        grid=(bsz, n_cb),
        in_specs=[
            pl.BlockSpec((None, seq, d), lambda b, c: (b, 0, 0)),
            blk(0), blk(1), blk(2), blk(3),
            pl.BlockSpec((CONV_K, CONV_CB), lambda b, c: (0, c)),
            pl.BlockSpec((1, CONV_CB), lambda b, c: (0, c)),
            pl.BlockSpec((width, d), lambda b, c: (0, 0)),
        ],
        out_specs=pl.BlockSpec((None, seq, d), lambda b, c: (b, 0, 0)),
        out_shape=jax.ShapeDtypeStruct((bsz, seq, d), BF16),
        scratch_shapes=[pltpu.VMEM((seq, d), BF16), pltpu.VMEM((n_cb, seq, CONV_CB), BF16)],
        compiler_params=pltpu.CompilerParams(
            dimension_semantics=("arbitrary", "arbitrary"), vmem_limit_bytes=VMEM_LIMIT_BYTES),
        name="conv_branch",
    )(x, w_in, w_in, w_in, w_in, conv_w, conv_b, w_proj_a)


N_PROJ = 7


def _proj_kernel(x_ref, wq_ref, w_ref, o_ref, xb_scr):
    j = pl.program_id(1)

    @pl.when(j == 0)
    def _():
        xb_scr[...] = x_ref[...].astype(BF16)

    @pl.when(j < 2)
    def _():
        r = _dot(xb_scr[...], wq_ref[...]) * (HEAD_DIM ** -0.5 * LOG2E)
        lane = lax.broadcasted_iota(jnp.int32, (r.shape[0], V_DIM), 1)
        parts = []
        for h in range(r.shape[1] // V_DIM):
            rh = r[:, h * V_DIM:(h + 1) * V_DIM]
            parts.append(jnp.where(lane < HEAD_DIM, rh, 0.0))
            parts.append(jnp.where(lane >= HEAD_DIM, rh, 0.0))
        o_ref[...] = jnp.concatenate(parts, axis=1).astype(BF16)

    @pl.when(jnp.logical_or(j == 2, j == 3))
    def _():
        o_ref[...] = _dot(xb_scr[...], w_ref[...]).astype(BF16)

    @pl.when(j == 4)
    def _():
        o_ref[...] = _silu(_dot(xb_scr[...], w_ref[...])).astype(BF16)

    @pl.when(j >= 5)
    def _():
        o_ref[...] = _sigmoid(_dot(xb_scr[...], w_ref[...])).astype(BF16)


def _proj(x, w_in):
    bsz, seq, d = x.shape
    return pl.pallas_call(
        _proj_kernel,
        grid=(bsz, N_PROJ),
        in_specs=[
            pl.BlockSpec((None, seq, d), lambda b, j: (b, 0, 0)),
            pl.BlockSpec((d, d // 2), lambda b, j: (0, 8 + jnp.minimum(j, 1))),
            pl.BlockSpec((d, d), lambda b, j: (0, 3 + jnp.maximum(j, 2))),
        ],
        out_specs=pl.BlockSpec((None, None, seq, d), lambda b, j: (j, b, 0, 0)),
        out_shape=jax.ShapeDtypeStruct((N_PROJ, bsz, seq, d), BF16),
        scratch_shapes=[pltpu.VMEM((seq, d), BF16)],
        compiler_params=pltpu.CompilerParams(
            dimension_semantics=("arbitrary", "arbitrary"), vmem_limit_bytes=VMEM_LIMIT_BYTES),
        name="proj",
    )(x, w_in, w_in)


def _diff_attn_kernel(lambda_init, tq, slopes_ref, q_ref, k_ref, v_ref, lq1_ref, lk1_ref,
                      lq2_ref, lk2_ref, g_ref, o_ref, bias_scr, vt_scr, qs_scr, s0_scr, s1_scr,
                      p0_scr, p1_scr):
    s_scr = (s0_scr, s1_scr)
    p_scr = (p0_scr, p1_scr)
    head = pl.program_id(0)
    b = pl.program_id(1)
    seq = k_ref.shape[0]
    nq = seq // tq
    kc = min(ATTN_KC, seq)
    n_kc = seq // kc
    sub = 8

    @pl.when(b == 0)
    def _():
        r = lax.broadcasted_iota(jnp.int32, bias_scr.shape, 0)
        i = lax.broadcasted_iota(jnp.int32, bias_scr.shape, 1)
        dist = jnp.abs(r - (seq - tq) - i).astype(F32)
        bias_scr[...] = dist * (-LOG2E * slopes_ref[head])

        vt_scr[V_DIM:V_DIM + 16, :] = jnp.where(
            lax.broadcasted_iota(jnp.int32, (16, seq), 0) == 0, 1.0, 0.0).astype(BF16)

    vt_scr[0:V_DIM, :] = v_ref[...].T

    lam = (jnp.exp(jnp.sum(lq1_ref[...] * lk1_ref[...], keepdims=True))
           - jnp.exp(jnp.sum(lq2_ref[...] * lk2_ref[...], keepdims=True))
           + lambda_init)

    def stage_queries(t):
        q = q_ref[t * tq:(t + 1) * tq, :]
        qs_scr[...] = jnp.concatenate([q[:, 0:V_DIM], q[:, V_DIM:2 * V_DIM]], axis=0)

    def phase_a_chunk(t, i):
        r0 = i * kc
        s = lax.dot_general(k_ref[r0:r0 + kc, :], qs_scr[...], (((1,), (1,)), ((), ())),
                            preferred_element_type=F32)
        b0 = (seq - tq) - t * tq + r0
        bias = bias_scr[b0:b0 + kc, :]
        s_scr[t % 2][r0:r0 + kc, :] = s + jnp.concatenate([bias, bias], axis=1)

    def logits_chunk(t, i):
        return s_scr[t % 2][i * kc:(i + 1) * kc, :].reshape(kc // sub, sub, 2 * tq)

    def column_max(t):
        m8 = jnp.full((sub, 2 * tq), -jnp.inf, F32)
        for i in range(n_kc):
            m8 = jnp.maximum(m8, jnp.max(logits_chunk(t, i), axis=0))
        return jnp.broadcast_to(jnp.max(m8, axis=0, keepdims=True), m8.shape)

    def phase_b_chunk(t, i, m8):
        r0 = i * kc
        p = jnp.exp2(logits_chunk(t, i) - m8[None])
        p_scr[t % 2][r0:r0 + kc, :] = p.reshape(kc, 2 * tq).astype(BF16)

    def finalize(t):
        ot = _dot(vt_scr[...], p_scr[t % 2][...])
        ot = ot[0:V_DIM, :] / ot[V_DIM:V_DIM + 1, :]
        odt = ot[:, 0:tq] - lam * ot[:, tq:2 * tq]
        ms = jnp.mean(odt * odt, axis=0, keepdims=True)
        ont = odt * lax.rsqrt(ms + RMS_EPS) * g_ref[...] * (1.0 - lambda_init)
        o_ref[t * tq:(t + 1) * tq, :] = ont.T.astype(BF16)

    for t in range(-1, nq + 1):
        if 0 <= t - 1 < nq:
            finalize(t - 1)
        if t + 1 < nq:
            stage_queries(t + 1)
        if 0 <= t < nq:
            m8 = column_max(t)
        for i in range(n_kc):
            if t + 1 < nq:
                phase_a_chunk(t + 1, i)
            if 0 <= t < nq:
                phase_b_chunk(t, i, m8)


def _diff_attn(proj, slopes, lq1, lk1, lq2, lk2, subln_g, lambda_init):
    _, bsz, seq, d = proj.shape
    tq = min(ATTN_TQ, seq)
    hpg = d // (2 * V_DIM)
    vec = lambda n: pl.BlockSpec((1, n), lambda h, b: (0, 0))
    return pl.pallas_call(
        functools.partial(_diff_attn_kernel, lambda_init, tq),
        grid=(N_HEADS, bsz),
        in_specs=[
            pl.BlockSpec(memory_space=pltpu.SMEM),
            pl.BlockSpec((None, None, seq, 2 * V_DIM), lambda h, b: (h // hpg, b, 0, h % hpg)),
            pl.BlockSpec((None, None, seq, V_DIM), lambda h, b: (2, b, 0, h)),
            pl.BlockSpec((None, None, seq, V_DIM), lambda h, b: (3, b, 0, h)),
            vec(HEAD_DIM), vec(HEAD_DIM), vec(HEAD_DIM), vec(HEAD_DIM),
            pl.BlockSpec((V_DIM, 1), lambda h, b: (0, 0)),
        ],
        out_specs=pl.BlockSpec((None, seq, V_DIM), lambda h, b: (b, 0, h)),
        out_shape=jax.ShapeDtypeStruct((bsz, seq, d), BF16),
        scratch_shapes=[
            pltpu.VMEM((2 * seq - tq, tq), F32),
            pltpu.VMEM((V_DIM + 16, seq), BF16),
            pltpu.VMEM((2 * tq, V_DIM), BF16),
            pltpu.VMEM((seq, 2 * tq), F32),
            pltpu.VMEM((seq, 2 * tq), F32),
            pltpu.VMEM((seq, 2 * tq), BF16),
            pltpu.VMEM((seq, 2 * tq), BF16),
        ],
        compiler_params=pltpu.CompilerParams(
            dimension_semantics=("arbitrary", "arbitrary"),
            vmem_limit_bytes=VMEM_LIMIT_BYTES),
        name="diff_attn",
    )(slopes, proj, proj, proj, lq1, lk1, lq2, lk2, subln_g)


def _epilogue_kernel(alpha, on_ref, zs_ref, ga_ref, gb_ref, ya_ref, x_ref, p_ref, wpb_ref,
                     wout_ref, wgate_ref, wple_ref, lng_ref, lnb_ref, out_ref):
    rows = on_ref.shape[0] // EPI_SPLIT
    for h in range(EPI_SPLIT):
        r = slice(h * rows, (h + 1) * rows)
        ob = on_ref[r, :] * zs_ref[r, :]
        yb = _dot(ob, wpb_ref[...])
        merged = (ga_ref[r, :].astype(F32) * ya_ref[r, :].astype(F32)
                  + gb_ref[r, :].astype(F32) * yb)
        res = alpha * x_ref[r, :] + _dot(merged.astype(BF16), wout_ref[...])
        gate = _sigmoid(_dot(res.astype(BF16), wgate_ref[...]))
        res = res + gate * _dot(p_ref[r, :].astype(BF16), wple_ref[...])
        mu = jnp.mean(res, axis=-1, keepdims=True)
        rc = res - mu
        var = jnp.mean(rc * rc, axis=-1, keepdims=True)
        out_ref[r, :] = rc * lax.rsqrt(var + LN_EPS) * lng_ref[...] + lnb_ref[...]


def _epilogue(on, proj, ya, x, p, w_proj_b, w_out, w_gate, w_ple, ln_g, ln_b, alpha):
    n, d = x.shape
    ple = p.shape[1]
    t = min(EPI_T, n)
    nt = n // t
    row = lambda w: pl.BlockSpec((t, w), lambda i: (i, 0))
    full = lambda r, c: pl.BlockSpec((r, c), lambda i: (0, 0))
    return pl.pallas_call(
        functools.partial(_epilogue_kernel, alpha),
        grid=(nt,),
        in_specs=[
            row(d),
            pl.BlockSpec((None, t, d), lambda i: (4, i, 0)),
            pl.BlockSpec((None, t, d), lambda i: (5, i, 0)),
            pl.BlockSpec((None, t, d), lambda i: (6, i, 0)),
            row(d), row(d), row(ple),
            full(d, d), full(d, d), full(d, d), full(ple, d), full(1, d), full(1, d),
        ],
        out_specs=row(d),
        out_shape=jax.ShapeDtypeStruct((n, d), F32),
        compiler_params=pltpu.CompilerParams(
            dimension_semantics=("arbitrary",), vmem_limit_bytes=VMEM_LIMIT_BYTES),
        name="epilogue",
    )(on, proj, proj, proj, ya, x, p, w_proj_b, w_out, w_gate, w_ple, ln_g, ln_b)


def kernel(x, p, w_in, conv_w, conv_b, w_proj_a, lambda_q1, lambda_k1, lambda_q2, lambda_k2,
           subln_g, w_proj_b, w_out, w_ple, w_ple_gate, ln_g, ln_b):
    depth = w_in.shape[0]
    bsz, seq, d = x.shape
    alpha = (2.0 * depth) ** 0.25
    slopes = jnp.asarray(
        np.float32(2.0) ** (-8.0 * np.arange(1, N_HEADS + 1, dtype=np.float32) / N_HEADS))
    h = x
    for i in range(depth):
        lambda_init = 0.8 - 0.6 * math.exp(-0.3 * i)
        w_in_b = w_in[i].astype(BF16)
        ya = _conv_branch(h, w_in_b, conv_w[i], conv_b[i][None, :], w_proj_a[i].astype(BF16))
        proj = _proj(h, w_in_b)
        on = _diff_attn(proj, slopes, lambda_q1[i][None, :], lambda_k1[i][None, :],
                        lambda_q2[i][None, :], lambda_k2[i][None, :], subln_g[i][:, None],
                        lambda_init)
        out = _epilogue(on.reshape(bsz * seq, d), proj.reshape(N_PROJ, bsz * seq, d),
                        ya.reshape(bsz * seq, d), h.reshape(bsz * seq, d),
                        p[i].reshape(bsz * seq, -1), w_proj_b[i].astype(BF16),
                        w_out[i].astype(BF16), w_ple_gate[i].astype(BF16),
                        w_ple[i].astype(BF16), ln_g[i][None, :], ln_b[i][None, :], alpha)
        h = out.reshape(bsz, seq, d)
    return h
```

```python
import functools
import math

import jax
import jax.numpy as jnp
import numpy as np
from jax import lax
from jax.experimental import pallas as pl
from jax.experimental.pallas import tpu as pltpu

N_HEADS = 8
HEAD_DIM = 64
V_DIM = 2 * HEAD_DIM
LN_EPS = 1e-5
RMS_EPS = 1e-5
CONV_K = 3
LOG2E = math.log2(math.e)

VMEM_LIMIT_BYTES = 56 * 1024 * 1024

N_PROJ = 6
PROJ_T = 1024
CONV_CB = 256
ATTN_TQ = 256
ATTN_KC = 256
EPI_T = 1024
EPI_SPLIT = 4

BF16 = jnp.bfloat16
F32 = jnp.float32


def _dot(a, b):
    return jnp.dot(a, b, preferred_element_type=F32)


def _sigmoid(x):
    return 0.5 * jnp.tanh(0.5 * x) + 0.5


def _silu(x):
    return x * _sigmoid(x)


def _proj_kernel(x_ref, wq_ref, wk_ref, wv_ref, wz_ref, wga_ref, wgb_ref, o_ref):
    xb = x_ref[...].astype(BF16)
    o_ref[0] = (_dot(xb, wq_ref[...]) * (HEAD_DIM ** -0.5 * LOG2E)).astype(BF16)
    o_ref[1] = _dot(xb, wk_ref[...]).astype(BF16)
    o_ref[2] = _dot(xb, wv_ref[...]).astype(BF16)
    o_ref[3] = _silu(_dot(xb, wz_ref[...])).astype(BF16)
    o_ref[4] = _sigmoid(_dot(xb, wga_ref[...])).astype(BF16)
    o_ref[5] = _sigmoid(_dot(xb, wgb_ref[...])).astype(BF16)
    o_ref[6] = xb


def _proj(x, w_in):
    n, d = x.shape
    t = min(PROJ_T, n)
    wspec = lambda g: pl.BlockSpec((d, d), lambda i, g=g: (0, 4 + g), pipeline_mode=pl.Buffered(1))
    return pl.pallas_call(
        _proj_kernel,
        grid=(n // t,),
        in_specs=[pl.BlockSpec((t, d), lambda i: (i, 0))] + [wspec(g) for g in range(N_PROJ)],
        out_specs=pl.BlockSpec((N_PROJ + 1, t, d), lambda i: (0, i, 0)),
        out_shape=jax.ShapeDtypeStruct((N_PROJ + 1, n, d), BF16),
        compiler_params=pltpu.CompilerParams(
            dimension_semantics=("arbitrary",), vmem_limit_bytes=VMEM_LIMIT_BYTES),
        name="proj",
    )(x, *([w_in] * N_PROJ))


def _conv_branch_kernel(x_ref, wu_ref, wc_ref, wb_ref, wz_ref, cw_ref, cb_ref, wpa_ref, ya_ref,
                        y_scr):
    seq = x_ref.shape[0]
    width = wpa_ref.shape[0]
    xb = x_ref[...]
    row = lax.broadcasted_iota(jnp.int32, (seq, CONV_CB), 0)
    for c0 in range(0, width, CONV_CB):
        cols = slice(c0, c0 + CONV_CB)
        h = _dot(xb, wc_ref[:, cols]) * _dot(xb, wu_ref[:, cols])
        h_prev = jnp.where(row == 0, 0.0, pltpu.roll(h, 1, axis=0))
        h_next = jnp.where(row == seq - 1, 0.0, pltpu.roll(h, seq - 1, axis=0))
        conv = (cw_ref[0:1, cols] * h_prev + cw_ref[1:2, cols] * h + cw_ref[2:3, cols] * h_next
                + cb_ref[:, cols])
        y = _dot(xb, wb_ref[:, cols]) * conv * _silu(_dot(xb, wz_ref[:, cols]))
        y_scr[:, cols] = y.astype(BF16)
    ya_ref[...] = _dot(y_scr[...], wpa_ref[...]).astype(BF16)


def _conv_branch(proj, w_in, conv_w, conv_b, w_proj_a):
    _, bsz, seq, d = proj.shape
    width = w_proj_a.shape[0]
    wspec = lambda g: pl.BlockSpec((d, width), lambda b, g=g: (0, g),
                                   pipeline_mode=pl.Buffered(1))
    const = lambda shape: pl.BlockSpec(shape, lambda b: (0, 0), pipeline_mode=pl.Buffered(1))
    return pl.pallas_call(
        _conv_branch_kernel,
        grid=(bsz,),
        in_specs=[
            pl.BlockSpec((None, None, seq, d), lambda b: (N_PROJ, b, 0, 0)),
            wspec(0), wspec(1), wspec(2), wspec(3),
            const((CONV_K, width)), const((1, width)), const((width, d)),
        ],
        out_specs=pl.BlockSpec((None, seq, d), lambda b: (b, 0, 0)),
        out_shape=jax.ShapeDtypeStruct((bsz, seq, d), BF16),
        scratch_shapes=[pltpu.VMEM((seq, width), BF16)],
        compiler_params=pltpu.CompilerParams(
            dimension_semantics=("arbitrary",), vmem_limit_bytes=VMEM_LIMIT_BYTES),
        name="conv_branch",
    )(proj, w_in, w_in, w_in, w_in, conv_w, conv_b, w_proj_a)


def _diff_attn_kernel(lambda_init, tq, slopes_ref, q_ref, k_ref, v_ref, lq1_ref, lk1_ref,
                      lq2_ref, lk2_ref, g_ref, o_ref, bias_scr, vt_scr, qs_scr, s0_scr, s1_scr,
                      p0_scr, p1_scr):
    s_scr = (s0_scr, s1_scr)
    p_scr = (p0_scr, p1_scr)
    head = pl.program_id(0)
    b = pl.program_id(1)
    seq = k_ref.shape[0]
    nq = seq // tq
    kc = min(ATTN_KC, seq)
    n_kc = seq // kc
    sub = 8

    @pl.when(b == 0)
    def _():
        r = lax.broadcasted_iota(jnp.int32, bias_scr.shape, 0)
        i = lax.broadcasted_iota(jnp.int32, bias_scr.shape, 1)
        dist = jnp.abs(r - (seq - tq) - i).astype(F32)
        bias_scr[...] = dist * (-LOG2E * slopes_ref[head])

        vt_scr[V_DIM:V_DIM + 16, :] = jnp.where(
            lax.broadcasted_iota(jnp.int32, (16, seq), 0) == 0, 1.0, 0.0).astype(BF16)

    vt_scr[0:V_DIM, :] = v_ref[...].T

    lam = (jnp.exp(jnp.sum(lq1_ref[...] * lk1_ref[...], keepdims=True))
           - jnp.exp(jnp.sum(lq2_ref[...] * lk2_ref[...], keepdims=True))
           + lambda_init)

    lane = lax.broadcasted_iota(jnp.int32, (1, V_DIM), 1)
    map1_lanes = jnp.where(lane < HEAD_DIM, 1.0, 0.0).astype(BF16)
    map2_lanes = jnp.where(lane >= HEAD_DIM, 1.0, 0.0).astype(BF16)

    def stage_queries(t):
        q = q_ref[t * tq:(t + 1) * tq, :]
        qs_scr[...] = jnp.concatenate([q * map1_lanes, q * map2_lanes], axis=0)

    def phase_a_chunk(t, i):
        r0 = i * kc
        s = lax.dot_general(k_ref[r0:r0 + kc, :], qs_scr[...], (((1,), (1,)), ((), ())),
                            preferred_element_type=F32)
        b0 = (seq - tq) - t * tq + r0
        bias = bias_scr[b0:b0 + kc, :]
        s_scr[t % 2][r0:r0 + kc, :] = s + jnp.concatenate([bias, bias], axis=1)

    def logits_chunk(t, i):
        return s_scr[t % 2][i * kc:(i + 1) * kc, :].reshape(kc // sub, sub, 2 * tq)

    def column_max(t):
        m8 = jnp.full((sub, 2 * tq), -jnp.inf, F32)
        for i in range(n_kc):
            m8 = jnp.maximum(m8, jnp.max(logits_chunk(t, i), axis=0))
        return jnp.broadcast_to(jnp.max(m8, axis=0, keepdims=True), m8.shape)

    def phase_b_chunk(t, i, m8):
        r0 = i * kc
        p = jnp.exp2(logits_chunk(t, i) - m8[None])
        p_scr[t % 2][r0:r0 + kc, :] = p.reshape(kc, 2 * tq).astype(BF16)

    def finalize(t):
        ot = _dot(vt_scr[...], p_scr[t % 2][...])
        ot = ot[0:V_DIM, :] / ot[V_DIM:V_DIM + 1, :]
        odt = ot[:, 0:tq] - lam * ot[:, tq:2 * tq]
        ms = jnp.mean(odt * odt, axis=0, keepdims=True)
        ont = odt * lax.rsqrt(ms + RMS_EPS) * g_ref[...] * (1.0 - lambda_init)
        o_ref[t * tq:(t + 1) * tq, :] = ont.T.astype(BF16)

    for t in range(-1, nq + 1):
        if 0 <= t - 1 < nq:
            finalize(t - 1)
        if t + 1 < nq:
            stage_queries(t + 1)
        if 0 <= t < nq:
            m8 = column_max(t)
        for i in range(n_kc):
            if t + 1 < nq:
                phase_a_chunk(t + 1, i)
            if 0 <= t < nq:
                phase_b_chunk(t, i, m8)


def _diff_attn(proj, slopes, lq1, lk1, lq2, lk2, subln_g, lambda_init):
    _, bsz, seq, d = proj.shape
    tq = min(ATTN_TQ, seq)
    vec = lambda n: pl.BlockSpec((1, n), lambda h, b: (0, 0))
    return pl.pallas_call(
        functools.partial(_diff_attn_kernel, lambda_init, tq),
        grid=(N_HEADS, bsz),
        in_specs=[
            pl.BlockSpec(memory_space=pltpu.SMEM),
            pl.BlockSpec((None, None, seq, V_DIM), lambda h, b: (0, b, 0, h)),
            pl.BlockSpec((None, None, seq, V_DIM), lambda h, b: (1, b, 0, h)),
            pl.BlockSpec((None, None, seq, V_DIM), lambda h, b: (2, b, 0, h)),
            vec(HEAD_DIM), vec(HEAD_DIM), vec(HEAD_DIM), vec(HEAD_DIM),
            pl.BlockSpec((V_DIM, 1), lambda h, b: (0, 0)),
        ],
        out_specs=pl.BlockSpec((None, seq, V_DIM), lambda h, b: (b, 0, h)),
        out_shape=jax.ShapeDtypeStruct((bsz, seq, d), BF16),
        scratch_shapes=[
            pltpu.VMEM((2 * seq - tq, tq), F32),
            pltpu.VMEM((V_DIM + 16, seq), BF16),
            pltpu.VMEM((2 * tq, V_DIM), BF16),
            pltpu.VMEM((seq, 2 * tq), F32),
            pltpu.VMEM((seq, 2 * tq), F32),
            pltpu.VMEM((seq, 2 * tq), BF16),
            pltpu.VMEM((seq, 2 * tq), BF16),
        ],
        compiler_params=pltpu.CompilerParams(
            dimension_semantics=("arbitrary", "arbitrary"),
            vmem_limit_bytes=VMEM_LIMIT_BYTES),
        name="diff_attn",
    )(slopes, proj, proj, proj, lq1, lk1, lq2, lk2, subln_g)


def _epilogue_kernel(alpha, on_ref, zs_ref, ga_ref, gb_ref, ya_ref, x_ref, p_ref, wpb_ref,
                     wout_ref, wgate_ref, wple_ref, lng_ref, lnb_ref, out_ref):
    rows = on_ref.shape[0] // EPI_SPLIT
    for h in range(EPI_SPLIT):
        r = slice(h * rows, (h + 1) * rows)
        ob = on_ref[r, :] * zs_ref[r, :]
        yb = _dot(ob, wpb_ref[...])
        merged = (ga_ref[r, :].astype(F32) * ya_ref[r, :].astype(F32)
                  + gb_ref[r, :].astype(F32) * yb)
        res = alpha * x_ref[r, :] + _dot(merged.astype(BF16), wout_ref[...])
        gate = _sigmoid(_dot(res.astype(BF16), wgate_ref[...]))
        res = res + gate * _dot(p_ref[r, :].astype(BF16), wple_ref[...])
        mu = jnp.mean(res, axis=-1, keepdims=True)
        rc = res - mu
        var = jnp.mean(rc * rc, axis=-1, keepdims=True)
        out_ref[r, :] = rc * lax.rsqrt(var + LN_EPS) * lng_ref[...] + lnb_ref[...]


def _epilogue(on, proj, ya, x, p, w_proj_b, w_out, w_gate, w_ple, ln_g, ln_b, alpha):
    n, d = x.shape
    ple = p.shape[1]
    t = min(EPI_T, n)
    nt = n // t
    row = lambda w: pl.BlockSpec((t, w), lambda i: (i, 0))
    full = lambda r, c: pl.BlockSpec((r, c), lambda i: (0, 0))
    return pl.pallas_call(
        functools.partial(_epilogue_kernel, alpha),
        grid=(nt,),
        in_specs=[
            row(d),
            pl.BlockSpec((None, t, d), lambda i: (3, i, 0)),
            pl.BlockSpec((None, t, d), lambda i: (4, i, 0)),
            pl.BlockSpec((None, t, d), lambda i: (5, i, 0)),
            row(d), row(d), row(ple),
            full(d, d), full(d, d), full(d, d), full(ple, d), full(1, d), full(1, d),
        ],
        out_specs=row(d),
        out_shape=jax.ShapeDtypeStruct((n, d), F32),
        compiler_params=pltpu.CompilerParams(
            dimension_semantics=("arbitrary",), vmem_limit_bytes=VMEM_LIMIT_BYTES),
        name="epilogue",
    )(on, proj, proj, proj, ya, x, p, w_proj_b, w_out, w_gate, w_ple, ln_g, ln_b)


def kernel(x, p, w_in, conv_w, conv_b, w_proj_a, lambda_q1, lambda_k1, lambda_q2, lambda_k2,
           subln_g, w_proj_b, w_out, w_ple, w_ple_gate, ln_g, ln_b):
    depth = w_in.shape[0]
    bsz, seq, d = x.shape
    alpha = (2.0 * depth) ** 0.25
    slopes = jnp.asarray(
        np.float32(2.0) ** (-8.0 * np.arange(1, N_HEADS + 1, dtype=np.float32) / N_HEADS))
    h = x
    for i in range(depth):
        lambda_init = 0.8 - 0.6 * math.exp(-0.3 * i)
        w_in_b = w_in[i].astype(BF16)
        proj = _proj(h.reshape(bsz * seq, d), w_in_b).reshape(N_PROJ + 1, bsz, seq, d)
        ya = _conv_branch(proj, w_in_b, conv_w[i], conv_b[i][None, :], w_proj_a[i].astype(BF16))
        on = _diff_attn(proj, slopes, lambda_q1[i][None, :], lambda_k1[i][None, :],
                        lambda_q2[i][None, :], lambda_k2[i][None, :], subln_g[i][:, None],
                        lambda_init)
        out = _epilogue(on.reshape(bsz * seq, d), proj.reshape(N_PROJ + 1, bsz * seq, d),
                        ya.reshape(bsz * seq, d), h.reshape(bsz * seq, d),
                        p[i].reshape(bsz * seq, -1), w_proj_b[i].astype(BF16),
                        w_out[i].astype(BF16), w_ple_gate[i].astype(BF16),
                        w_ple[i].astype(BF16), ln_g[i][None, :], ln_b[i][None, :], alpha)
        h = out.reshape(bsz, seq, d)
    return h
```

```python
import functools
import math

import jax
import jax.numpy as jnp
import numpy as np
from jax import lax
from jax.experimental import pallas as pl
from jax.experimental.pallas import tpu as pltpu

N_HEADS = 8
HEAD_DIM = 64
V_DIM = 2 * HEAD_DIM
LN_EPS = 1e-5
RMS_EPS = 1e-5
CONV_K = 3
LOG2E = math.log2(math.e)

VMEM_LIMIT_BYTES = 56 * 1024 * 1024

N_PROJ = 6
PROJ_T = 1024
CONV_CB = 256
ATTN_TQ = 256
ATTN_KC = 256
ATTN_BANDS = (1, 2, 4)
UNDERFLOW = 130.0
EPI_T = 1024
EPI_SPLIT = 4

BF16 = jnp.bfloat16
F32 = jnp.float32


def _dot(a, b):
    return jnp.dot(a, b, preferred_element_type=F32)


def _sigmoid(x):
    return 0.5 * jnp.tanh(0.5 * x) + 0.5


def _silu(x):
    return x * _sigmoid(x)


def _proj_kernel(x_ref, wq_ref, wk_ref, wv_ref, wz_ref, wga_ref, wgb_ref, o_ref):
    xb = x_ref[...].astype(BF16)
    o_ref[0] = (_dot(xb, wq_ref[...]) * (HEAD_DIM ** -0.5 * LOG2E)).astype(BF16)
    o_ref[1] = _dot(xb, wk_ref[...]).astype(BF16)
    o_ref[2] = _dot(xb, wv_ref[...]).astype(BF16)
    o_ref[3] = _silu(_dot(xb, wz_ref[...])).astype(BF16)
    o_ref[4] = _sigmoid(_dot(xb, wga_ref[...])).astype(BF16)
    o_ref[5] = _sigmoid(_dot(xb, wgb_ref[...])).astype(BF16)
    o_ref[6] = xb


def _proj(x, w_in):
    n, d = x.shape
    t = min(PROJ_T, n)
    wspec = lambda g: pl.BlockSpec((d, d), lambda i, g=g: (0, 4 + g), pipeline_mode=pl.Buffered(1))
    return pl.pallas_call(
        _proj_kernel,
        grid=(n // t,),
        in_specs=[pl.BlockSpec((t, d), lambda i: (i, 0))] + [wspec(g) for g in range(N_PROJ)],
        out_specs=pl.BlockSpec((N_PROJ + 1, t, d), lambda i: (0, i, 0)),
        out_shape=jax.ShapeDtypeStruct((N_PROJ + 1, n, d), BF16),
        compiler_params=pltpu.CompilerParams(
            dimension_semantics=("arbitrary",), vmem_limit_bytes=VMEM_LIMIT_BYTES),
        name="proj",
    )(x, *([w_in] * N_PROJ))


def _conv_branch_kernel(x_ref, wu_ref, wc_ref, wb_ref, wz_ref, cw_ref, cb_ref, wpa_ref, ya_ref,
                        y_scr):
    seq = x_ref.shape[0]
    width = wpa_ref.shape[0]
    xb = x_ref[...]
    row = lax.broadcasted_iota(jnp.int32, (seq, CONV_CB), 0)
    for c0 in range(0, width, CONV_CB):
        cols = slice(c0, c0 + CONV_CB)
        h = _dot(xb, wc_ref[:, cols]) * _dot(xb, wu_ref[:, cols])
        h_prev = jnp.where(row == 0, 0.0, pltpu.roll(h, 1, axis=0))
        h_next = jnp.where(row == seq - 1, 0.0, pltpu.roll(h, seq - 1, axis=0))
        conv = (cw_ref[0:1, cols] * h_prev + cw_ref[1:2, cols] * h + cw_ref[2:3, cols] * h_next
                + cb_ref[:, cols])
        y = _dot(xb, wb_ref[:, cols]) * conv * _silu(_dot(xb, wz_ref[:, cols]))
        y_scr[:, cols] = y.astype(BF16)
    ya_ref[...] = _dot(y_scr[...], wpa_ref[...]).astype(BF16)


def _conv_branch(proj, w_in, conv_w, conv_b, w_proj_a):
    _, bsz, seq, d = proj.shape
    width = w_proj_a.shape[0]
    wspec = lambda g: pl.BlockSpec((d, width), lambda b, g=g: (0, g),
                                   pipeline_mode=pl.Buffered(1))
    const = lambda shape: pl.BlockSpec(shape, lambda b: (0, 0), pipeline_mode=pl.Buffered(1))
    return pl.pallas_call(
        _conv_branch_kernel,
        grid=(bsz,),
        in_specs=[
            pl.BlockSpec((None, None, seq, d), lambda b: (N_PROJ, b, 0, 0)),
            wspec(0), wspec(1), wspec(2), wspec(3),
            const((CONV_K, width)), const((1, width)), const((width, d)),
        ],
        out_specs=pl.BlockSpec((None, seq, d), lambda b: (b, 0, 0)),
        out_shape=jax.ShapeDtypeStruct((bsz, seq, d), BF16),
        scratch_shapes=[pltpu.VMEM((seq, width), BF16)],
        compiler_params=pltpu.CompilerParams(
            dimension_semantics=("arbitrary",), vmem_limit_bytes=VMEM_LIMIT_BYTES),
        name="conv_branch",
    )(proj, w_in, w_in, w_in, w_in, conv_w, conv_b, w_proj_a)


def _diff_attn_kernel(lambda_init, tq, head0, bands, slopes_ref, q_ref, k_ref, v_ref, lq1_ref,
                      lk1_ref, lq2_ref, lk2_ref, g_ref, o_ref, *rest):
    if bands is None:
        ok_ref = None
        bias_scr, vt_scr, qs_scr, s0_scr, s1_scr, p0_scr, p1_scr = rest
    else:
        ok_ref, bias_scr, vt_scr, qs_scr, s0_scr, s1_scr, p0_scr, p1_scr = rest
    s_scr = (s0_scr, s1_scr)
    p_scr = (p0_scr, p1_scr)
    head = head0 + pl.program_id(0)
    b = pl.program_id(1)
    seq = k_ref.shape[0]
    nq = seq // tq
    kc = min(ATTN_KC, seq)
    n_kc = seq // kc
    sub = 8

    @pl.when(b == 0)
    def _():
        r = lax.broadcasted_iota(jnp.int32, bias_scr.shape, 0)
        i = lax.broadcasted_iota(jnp.int32, bias_scr.shape, 1)
        dist = jnp.abs(r - (seq - tq) - i).astype(F32)
        bias_scr[...] = dist * (-LOG2E * slopes_ref[head])

        vt_scr[V_DIM:V_DIM + 16, :] = jnp.where(
            lax.broadcasted_iota(jnp.int32, (16, seq), 0) == 0, 1.0, 0.0).astype(BF16)

    vt_scr[0:V_DIM, :] = v_ref[...].T

    lam = (jnp.exp(jnp.sum(lq1_ref[...] * lk1_ref[...], keepdims=True))
           - jnp.exp(jnp.sum(lq2_ref[...] * lk2_ref[...], keepdims=True))
           + lambda_init)

    lane = lax.broadcasted_iota(jnp.int32, (1, V_DIM), 1)
    map1_lanes = jnp.where(lane < HEAD_DIM, 1.0, 0.0).astype(BF16)
    map2_lanes = jnp.where(lane >= HEAD_DIM, 1.0, 0.0).astype(BF16)

    def stage_queries(t):
        q = q_ref[t * tq:(t + 1) * tq, :]
        qs_scr[...] = jnp.concatenate([q * map1_lanes, q * map2_lanes], axis=0)

    def phase_a_chunk(t, i):
        r0 = i * kc
        s = lax.dot_general(k_ref[r0:r0 + kc, :], qs_scr[...], (((1,), (1,)), ((), ())),
                            preferred_element_type=F32)
        b0 = (seq - tq) - t * tq + r0
        bias = bias_scr[b0:b0 + kc, :]
        s_scr[t % 2][r0:r0 + kc, :] = s + jnp.concatenate([bias, bias], axis=1)

    def logits_chunk(t, i):
        return s_scr[t % 2][i * kc:(i + 1) * kc, :].reshape(kc // sub, sub, 2 * tq)

    def column_max(t, chunks):
        m8 = jnp.full((sub, 2 * tq), -jnp.inf, F32)
        for i in chunks:
            m8 = jnp.maximum(m8, jnp.max(logits_chunk(t, i), axis=0))
        return jnp.broadcast_to(jnp.max(m8, axis=0, keepdims=True), m8.shape)

    def phase_b_chunk(t, i, m8):
        r0 = i * kc
        p = jnp.exp2(logits_chunk(t, i) - m8[None])
        p_scr[t % 2][r0:r0 + kc, :] = p.reshape(kc, 2 * tq).astype(BF16)

    def finalize(t, chunks):
        rows = slice(chunks[0] * kc, (chunks[-1] + 1) * kc)
        ot = _dot(vt_scr[:, rows], p_scr[t % 2][rows, :])
        ot = ot[0:V_DIM, :] / ot[V_DIM:V_DIM + 1, :]
        odt = ot[:, 0:tq] - lam * ot[:, tq:2 * tq]
        ms = jnp.mean(odt * odt, axis=0, keepdims=True)
        ont = odt * lax.rsqrt(ms + RMS_EPS) * g_ref[...] * (1.0 - lambda_init)
        o_ref[t * tq:(t + 1) * tq, :] = ont.T.astype(BF16)

    def pipeline(band):
        def chunks_of(t):
            return [i for i in range(n_kc) if band is None or abs(i * kc // tq - t) <= band]

        for t in range(-1, nq + 1):
            if 0 <= t - 1 < nq:
                finalize(t - 1, chunks_of(t - 1))
            if t + 1 < nq:
                stage_queries(t + 1)
            if 0 <= t < nq:
                m8 = column_max(t, chunks_of(t))
            for i in range(n_kc):
                if t + 1 < nq and i in chunks_of(t + 1):
                    phase_a_chunk(t + 1, i)
                if 0 <= t < nq and i in chunks_of(t):
                    phase_b_chunk(t, i, m8)

    if bands is None:
        pipeline(None)
        return

    qf = q_ref[...].astype(F32)
    kf = k_ref[...].astype(F32)
    lane_f = lax.broadcasted_iota(jnp.int32, (1, V_DIM), 1)
    sel = (jnp.where(lane_f < HEAD_DIM, 1.0, 0.0), jnp.where(lane_f >= HEAD_DIM, 1.0, 0.0))
    slope = LOG2E * slopes_ref[head]
    margins = []
    for msk in sel:
        qn = jnp.sum(qf * qf * msk, axis=1, keepdims=True)
        kn = jnp.max(jnp.sum(kf * kf * msk, axis=1, keepdims=True), axis=0, keepdims=True)
        diag = jnp.sum(qf * kf * msk, axis=1, keepdims=True)
        margins.append(jnp.max(jnp.sqrt(qn * kn) - diag, axis=0, keepdims=True))
    margin = jnp.maximum(margins[0], margins[1])
    for w, band in enumerate(bands):
        @pl.when(pl.program_id(0) == w)
        def _(band=band):
            reach = slope * (band * tq + 1)
            good = margin + UNDERFLOW <= reach
            ok_ref[...] = jnp.broadcast_to(jnp.where(good, 1.0, 0.0), ok_ref.shape)
            pipeline(band)


def _diff_attn(proj, slopes, lq1, lk1, lq2, lk2, subln_g, lambda_init, head0, n_heads, bands=None):
    _, bsz, seq, d = proj.shape
    tq = min(ATTN_TQ, seq)
    vec = lambda n: pl.BlockSpec((1, n), lambda h, b: (0, 0))
    out_specs = pl.BlockSpec((None, seq, V_DIM), lambda h, b: (b, 0, h))
    out_shape = jax.ShapeDtypeStruct((bsz, seq, n_heads * V_DIM), BF16)
    if bands is not None:
        assert len(bands) == n_heads
        out_specs = (out_specs, pl.BlockSpec((None, None, 8, 128), lambda h, b: (h, b, 0, 0)))
        out_shape = (out_shape, jax.ShapeDtypeStruct((n_heads, bsz, 8, 128), F32))
    return pl.pallas_call(
        functools.partial(_diff_attn_kernel, lambda_init, tq, head0, bands),
        grid=(n_heads, bsz),
        in_specs=[
            pl.BlockSpec(memory_space=pltpu.SMEM),
            pl.BlockSpec((None, None, seq, V_DIM), lambda h, b: (0, b, 0, head0 + h)),
            pl.BlockSpec((None, None, seq, V_DIM), lambda h, b: (1, b, 0, head0 + h)),
            pl.BlockSpec((None, None, seq, V_DIM), lambda h, b: (2, b, 0, head0 + h)),
            vec(HEAD_DIM), vec(HEAD_DIM), vec(HEAD_DIM), vec(HEAD_DIM),
            pl.BlockSpec((V_DIM, 1), lambda h, b: (0, 0)),
        ],
        out_specs=out_specs,
        out_shape=out_shape,
        scratch_shapes=[
            pltpu.VMEM((2 * seq - tq, tq), F32),
            pltpu.VMEM((V_DIM + 16, seq), BF16),
            pltpu.VMEM((2 * tq, V_DIM), BF16),
            pltpu.VMEM((seq, 2 * tq), F32),
            pltpu.VMEM((seq, 2 * tq), F32),
            pltpu.VMEM((seq, 2 * tq), BF16),
            pltpu.VMEM((seq, 2 * tq), BF16),
        ],
        compiler_params=pltpu.CompilerParams(
            dimension_semantics=("arbitrary", "arbitrary"),
            vmem_limit_bytes=VMEM_LIMIT_BYTES),
        name="diff_attn" if bands is None else "diff_attn_banded",
    )(slopes, proj, proj, proj, lq1, lk1, lq2, lk2, subln_g)


def _attention(proj, slopes, lq1, lk1, lq2, lk2, subln_g, lambda_init):
    args = (proj, slopes, lq1, lk1, lq2, lk2, subln_g, lambda_init)
    n_band = len(ATTN_BANDS)
    rest = _diff_attn(*args, n_band, N_HEADS - n_band)
    near, ok = _diff_attn(*args, 0, n_band, ATTN_BANDS)
    near = lax.cond(jnp.all(ok > 0.5), lambda: near, lambda: _diff_attn(*args, 0, n_band))
    return near, rest


def _epilogue_kernel(alpha, on_a_ref, on_b_ref, zs_ref, ga_ref, gb_ref, ya_ref, x_ref, p_ref,
                     wpb_ref, wout_ref, wgate_ref, wple_ref, lng_ref, lnb_ref, out_ref):
    rows = on_a_ref.shape[0] // EPI_SPLIT
    for h in range(EPI_SPLIT):
        r = slice(h * rows, (h + 1) * rows)
        ob = jnp.concatenate([on_a_ref[r, :], on_b_ref[r, :]], axis=1) * zs_ref[r, :]
        yb = _dot(ob, wpb_ref[...])
        merged = (ga_ref[r, :].astype(F32) * ya_ref[r, :].astype(F32)
                  + gb_ref[r, :].astype(F32) * yb)
        res = alpha * x_ref[r, :] + _dot(merged.astype(BF16), wout_ref[...])
        gate = _sigmoid(_dot(res.astype(BF16), wgate_ref[...]))
        res = res + gate * _dot(p_ref[r, :].astype(BF16), wple_ref[...])
        mu = jnp.mean(res, axis=-1, keepdims=True)
        rc = res - mu
        var = jnp.mean(rc * rc, axis=-1, keepdims=True)
        out_ref[r, :] = rc * lax.rsqrt(var + LN_EPS) * lng_ref[...] + lnb_ref[...]


def _epilogue(on_a, on_b, proj, ya, x, p, w_proj_b, w_out, w_gate, w_ple, ln_g, ln_b, alpha):
    n, d = x.shape
    ple = p.shape[1]
    t = min(EPI_T, n)
    nt = n // t
    row = lambda w: pl.BlockSpec((t, w), lambda i: (i, 0))
    full = lambda r, c: pl.BlockSpec((r, c), lambda i: (0, 0))
    return pl.pallas_call(
        functools.partial(_epilogue_kernel, alpha),
        grid=(nt,),
        in_specs=[
            row(on_a.shape[1]), row(on_b.shape[1]),
            pl.BlockSpec((None, t, d), lambda i: (3, i, 0)),
            pl.BlockSpec((None, t, d), lambda i: (4, i, 0)),
            pl.BlockSpec((None, t, d), lambda i: (5, i, 0)),
            row(d), row(d), row(ple),
            full(d, d), full(d, d), full(d, d), full(ple, d), full(1, d), full(1, d),
        ],
        out_specs=row(d),
        out_shape=jax.ShapeDtypeStruct((n, d), F32),
        compiler_params=pltpu.CompilerParams(
            dimension_semantics=("arbitrary",), vmem_limit_bytes=VMEM_LIMIT_BYTES),
        name="epilogue",
    )(on_a, on_b, proj, proj, proj, ya, x, p, w_proj_b, w_out, w_gate, w_ple, ln_g, ln_b)


def kernel(x, p, w_in, conv_w, conv_b, w_proj_a, lambda_q1, lambda_k1, lambda_q2, lambda_k2,
           subln_g, w_proj_b, w_out, w_ple, w_ple_gate, ln_g, ln_b):
    depth = w_in.shape[0]
    bsz, seq, d = x.shape
    alpha = (2.0 * depth) ** 0.25
    slopes = jnp.asarray(
        np.float32(2.0) ** (-8.0 * np.arange(1, N_HEADS + 1, dtype=np.float32) / N_HEADS))
    h = x
    for i in range(depth):
        lambda_init = 0.8 - 0.6 * math.exp(-0.3 * i)
        w_in_b = w_in[i].astype(BF16)
        proj = _proj(h.reshape(bsz * seq, d), w_in_b).reshape(N_PROJ + 1, bsz, seq, d)
        ya = _conv_branch(proj, w_in_b, conv_w[i], conv_b[i][None, :], w_proj_a[i].astype(BF16))
        on_a, on_b = _attention(proj, slopes, lambda_q1[i][None, :], lambda_k1[i][None, :],
                                lambda_q2[i][None, :], lambda_k2[i][None, :], subln_g[i][:, None],
                                lambda_init)
        out = _epilogue(on_a.reshape(bsz * seq, -1), on_b.reshape(bsz * seq, -1),
                        proj.reshape(N_PROJ + 1, bsz * seq, d),
                        ya.reshape(bsz * seq, d), h.reshape(bsz * seq, d),
                        p[i].reshape(bsz * seq, -1), w_proj_b[i].astype(BF16),
                        w_out[i].astype(BF16), w_ple_gate[i].astype(BF16),
                        w_ple[i].astype(BF16), ln_g[i][None, :], ln_b[i][None, :], alpha)
        h = out.reshape(bsz, seq, d)
    return h
```

```python
import functools
import math

import jax
import jax.numpy as jnp
import numpy as np
from jax import lax
from jax.experimental import pallas as pl
from jax.experimental.pallas import tpu as pltpu

N_HEADS = 8
HEAD_DIM = 64
V_DIM = 2 * HEAD_DIM
LN_EPS = 1e-5
RMS_EPS = 1e-5
CONV_K = 3
LOG2E = math.log2(math.e)

VMEM_LIMIT_BYTES = 56 * 1024 * 1024

N_PROJ = 6
PROJ_T = 1024
CONV_CB = 256
ATTN_TQ = 256
ATTN_KC = 256
ATTN_BANDS = (1, 2, 4)
UNDERFLOW = 130.0
BOUND_SLACK = 1.02
EPI_T = 1024
EPI_SPLIT = 4

BF16 = jnp.bfloat16
F32 = jnp.float32


def _dot(a, b):
    return jnp.dot(a, b, preferred_element_type=F32)


def _sigmoid(x):
    return 0.5 * jnp.tanh(0.5 * x) + 0.5


def _silu(x):
    return x * _sigmoid(x)


def _proj_kernel(x_ref, wq_ref, wk_ref, wv_ref, wz_ref, wga_ref, wgb_ref, o_ref):
    xb = x_ref[...].astype(BF16)
    o_ref[0] = (_dot(xb, wq_ref[...]) * (HEAD_DIM ** -0.5 * LOG2E)).astype(BF16)
    o_ref[1] = _dot(xb, wk_ref[...]).astype(BF16)
    o_ref[2] = _dot(xb, wv_ref[...]).astype(BF16)
    o_ref[3] = _silu(_dot(xb, wz_ref[...])).astype(BF16)
    o_ref[4] = _sigmoid(_dot(xb, wga_ref[...])).astype(BF16)
    o_ref[5] = _sigmoid(_dot(xb, wgb_ref[...])).astype(BF16)
    o_ref[6] = xb


def _proj(x, w_in):
    n, d = x.shape
    t = min(PROJ_T, n)
    wspec = lambda g: pl.BlockSpec((d, d), lambda i, g=g: (0, 4 + g), pipeline_mode=pl.Buffered(1))
    return pl.pallas_call(
        _proj_kernel,
        grid=(n // t,),
        in_specs=[pl.BlockSpec((t, d), lambda i: (i, 0))] + [wspec(g) for g in range(N_PROJ)],
        out_specs=pl.BlockSpec((N_PROJ + 1, t, d), lambda i: (0, i, 0)),
        out_shape=jax.ShapeDtypeStruct((N_PROJ + 1, n, d), BF16),
        compiler_params=pltpu.CompilerParams(
            dimension_semantics=("arbitrary",), vmem_limit_bytes=VMEM_LIMIT_BYTES),
        name="proj",
    )(x, *([w_in] * N_PROJ))


def _conv_branch_kernel(x_ref, wu_ref, wc_ref, wb_ref, wz_ref, cw_ref, cb_ref, wpa_ref, ya_ref,
                        y_scr):
    seq = x_ref.shape[0]
    width = wpa_ref.shape[0]
    xb = x_ref[...]
    row = lax.broadcasted_iota(jnp.int32, (seq, CONV_CB), 0)
    for c0 in range(0, width, CONV_CB):
        cols = slice(c0, c0 + CONV_CB)
        h = _dot(xb, wc_ref[:, cols]) * _dot(xb, wu_ref[:, cols])
        h_prev = jnp.where(row == 0, 0.0, pltpu.roll(h, 1, axis=0))
        h_next = jnp.where(row == seq - 1, 0.0, pltpu.roll(h, seq - 1, axis=0))
        conv = (cw_ref[0:1, cols] * h_prev + cw_ref[1:2, cols] * h + cw_ref[2:3, cols] * h_next
                + cb_ref[:, cols])
        y = _dot(xb, wb_ref[:, cols]) * conv * _silu(_dot(xb, wz_ref[:, cols]))
        y_scr[:, cols] = y.astype(BF16)
    ya_ref[...] = _dot(y_scr[...], wpa_ref[...]).astype(BF16)


def _conv_branch(proj, w_in, conv_w, conv_b, w_proj_a):
    _, bsz, seq, d = proj.shape
    width = w_proj_a.shape[0]
    wspec = lambda g: pl.BlockSpec((d, width), lambda b, g=g: (0, g),
                                   pipeline_mode=pl.Buffered(1))
    const = lambda shape: pl.BlockSpec(shape, lambda b: (0, 0), pipeline_mode=pl.Buffered(1))
    return pl.pallas_call(
        _conv_branch_kernel,
        grid=(bsz,),
        in_specs=[
            pl.BlockSpec((None, None, seq, d), lambda b: (N_PROJ, b, 0, 0)),
            wspec(0), wspec(1), wspec(2), wspec(3),
            const((CONV_K, width)), const((1, width)), const((width, d)),
        ],
        out_specs=pl.BlockSpec((None, seq, d), lambda b: (b, 0, 0)),
        out_shape=jax.ShapeDtypeStruct((bsz, seq, d), BF16),
        scratch_shapes=[pltpu.VMEM((seq, width), BF16)],
        compiler_params=pltpu.CompilerParams(
            dimension_semantics=("arbitrary",), vmem_limit_bytes=VMEM_LIMIT_BYTES),
        name="conv_branch",
    )(proj, w_in, w_in, w_in, w_in, conv_w, conv_b, w_proj_a)


def _diff_attn_kernel(lambda_init, tq, head0, bands, slopes_ref, q_ref, k_ref, v_ref, lq1_ref,
                      lk1_ref, lq2_ref, lk2_ref, g_ref, o_ref, *rest):
    if bands is None:
        ok_ref = None
        bias_scr, vt_scr, qs_scr, s0_scr, s1_scr, p0_scr, p1_scr = rest
    else:
        ok_ref, bias_scr, vt_scr, qs_scr, s0_scr, s1_scr, p0_scr, p1_scr = rest
    s_scr = (s0_scr, s1_scr)
    p_scr = (p0_scr, p1_scr)
    head = head0 + pl.program_id(0)
    b = pl.program_id(1)
    seq = k_ref.shape[0]
    nq = seq // tq
    kc = min(ATTN_KC, seq)
    n_kc = seq // kc
    sub = 8

    @pl.when(b == 0)
    def _():
        r = lax.broadcasted_iota(jnp.int32, bias_scr.shape, 0)
        i = lax.broadcasted_iota(jnp.int32, bias_scr.shape, 1)
        dist = jnp.abs(r - (seq - tq) - i).astype(F32)
        bias_scr[...] = dist * (-LOG2E * slopes_ref[head])

        vt_scr[V_DIM:V_DIM + 16, :] = jnp.where(
            lax.broadcasted_iota(jnp.int32, (16, seq), 0) == 0, 1.0, 0.0).astype(BF16)

    vt_scr[0:V_DIM, :] = v_ref[...].T

    lam = (jnp.exp(jnp.sum(lq1_ref[...] * lk1_ref[...], keepdims=True))
           - jnp.exp(jnp.sum(lq2_ref[...] * lk2_ref[...], keepdims=True))
           + lambda_init)

    lane = lax.broadcasted_iota(jnp.int32, (1, V_DIM), 1)
    map1_lanes = jnp.where(lane < HEAD_DIM, 1.0, 0.0).astype(BF16)
    map2_lanes = jnp.where(lane >= HEAD_DIM, 1.0, 0.0).astype(BF16)

    def stage_queries(t):
        q = q_ref[t * tq:(t + 1) * tq, :]
        qs_scr[...] = jnp.concatenate([q * map1_lanes, q * map2_lanes], axis=0)

    def phase_a_chunk(t, i):
        r0 = i * kc
        s = lax.dot_general(k_ref[r0:r0 + kc, :], qs_scr[...], (((1,), (1,)), ((), ())),
                            preferred_element_type=F32)
        b0 = (seq - tq) - t * tq + r0
        bias = bias_scr[b0:b0 + kc, :]
        s_scr[t % 2][r0:r0 + kc, :] = s + jnp.concatenate([bias, bias], axis=1)

    def logits_chunk(t, i):
        return s_scr[t % 2][i * kc:(i + 1) * kc, :].reshape(kc // sub, sub, 2 * tq)

    def column_max(t, chunks):
        m8 = jnp.full((sub, 2 * tq), -jnp.inf, F32)
        for i in chunks:
            m8 = jnp.maximum(m8, jnp.max(logits_chunk(t, i), axis=0))
        return jnp.broadcast_to(jnp.max(m8, axis=0, keepdims=True), m8.shape)

    def phase_b_chunk(t, i, m8):
        r0 = i * kc
        p = jnp.exp2(logits_chunk(t, i) - m8[None])
        p_scr[t % 2][r0:r0 + kc, :] = p.reshape(kc, 2 * tq).astype(BF16)

    def finalize(t, chunks):
        rows = slice(chunks[0] * kc, (chunks[-1] + 1) * kc)
        ot = _dot(vt_scr[:, rows], p_scr[t % 2][rows, :])
        ot = ot[0:V_DIM, :] / ot[V_DIM:V_DIM + 1, :]
        odt = ot[:, 0:tq] - lam * ot[:, tq:2 * tq]
        ms = jnp.mean(odt * odt, axis=0, keepdims=True)
        ont = odt * lax.rsqrt(ms + RMS_EPS) * g_ref[...] * (1.0 - lambda_init)
        o_ref[t * tq:(t + 1) * tq, :] = ont.T.astype(BF16)

    def pipeline(band):
        def chunks_of(t):
            return [i for i in range(n_kc) if band is None or abs(i * kc // tq - t) <= band]

        for t in range(-1, nq + 1):
            if 0 <= t - 1 < nq:
                finalize(t - 1, chunks_of(t - 1))
            if t + 1 < nq:
                stage_queries(t + 1)
            if 0 <= t < nq:
                m8 = column_max(t, chunks_of(t))
            for i in range(n_kc):
                if t + 1 < nq and i in chunks_of(t + 1):
                    phase_a_chunk(t + 1, i)
                if 0 <= t < nq and i in chunks_of(t):
                    phase_b_chunk(t, i, m8)

    if bands is None:
        pipeline(None)
        return

    qf = q_ref[...].astype(F32)
    kf = k_ref[...].astype(F32)
    terms = jnp.concatenate([(qf * qf).astype(BF16), (kf * kf).astype(BF16),
                             (qf * kf).astype(BF16)], axis=1)
    r = lax.broadcasted_iota(jnp.int32, (3 * V_DIM, V_DIM), 0)
    c = lax.broadcasted_iota(jnp.int32, (3 * V_DIM, V_DIM), 1)
    selector = jnp.where(c == r // HEAD_DIM, 1.0, 0.0).astype(BF16)
    sums = _dot(terms, selector)
    hi = jnp.max(sums, axis=0, keepdims=True)
    lo = jnp.min(sums, axis=0, keepdims=True)
    lane_s = lax.broadcasted_iota(jnp.int32, (1, V_DIM), 1)
    pick = lambda v, col: jnp.sum(jnp.where(lane_s == col, v, 0.0), axis=1, keepdims=True)
    slope = LOG2E * slopes_ref[head]
    margin = jnp.maximum(
        BOUND_SLACK * jnp.sqrt(pick(hi, 0) * pick(hi, 2)) - pick(lo, 4),
        BOUND_SLACK * jnp.sqrt(pick(hi, 1) * pick(hi, 3)) - pick(lo, 5))
    for w, band in enumerate(bands):
        @pl.when(pl.program_id(0) == w)
        def _(band=band):
            reach = slope * (band * tq + 1)
            good = margin + UNDERFLOW <= reach
            ok_ref[...] = jnp.broadcast_to(jnp.where(good, 1.0, 0.0), ok_ref.shape)
            pipeline(band)


def _diff_attn(proj, slopes, lq1, lk1, lq2, lk2, subln_g, lambda_init, head0, n_heads, bands=None):
    _, bsz, seq, d = proj.shape
    tq = min(ATTN_TQ, seq)
    vec = lambda n: pl.BlockSpec((1, n), lambda h, b: (0, 0))
    out_specs = pl.BlockSpec((None, seq, V_DIM), lambda h, b: (b, 0, h))
    out_shape = jax.ShapeDtypeStruct((bsz, seq, n_heads * V_DIM), BF16)
    if bands is not None:
        assert len(bands) == n_heads
        out_specs = (out_specs, pl.BlockSpec((None, None, 8, 128), lambda h, b: (h, b, 0, 0)))
        out_shape = (out_shape, jax.ShapeDtypeStruct((n_heads, bsz, 8, 128), F32))
    return pl.pallas_call(
        functools.partial(_diff_attn_kernel, lambda_init, tq, head0, bands),
        grid=(n_heads, bsz),
        in_specs=[
            pl.BlockSpec(memory_space=pltpu.SMEM),
            pl.BlockSpec((None, None, seq, V_DIM), lambda h, b: (0, b, 0, head0 + h)),
            pl.BlockSpec((None, None, seq, V_DIM), lambda h, b: (1, b, 0, head0 + h)),
            pl.BlockSpec((None, None, seq, V_DIM), lambda h, b: (2, b, 0, head0 + h)),
            vec(HEAD_DIM), vec(HEAD_DIM), vec(HEAD_DIM), vec(HEAD_DIM),
            pl.BlockSpec((V_DIM, 1), lambda h, b: (0, 0)),
        ],
        out_specs=out_specs,
        out_shape=out_shape,
        scratch_shapes=[
            pltpu.VMEM((2 * seq - tq, tq), F32),
            pltpu.VMEM((V_DIM + 16, seq), BF16),
            pltpu.VMEM((2 * tq, V_DIM), BF16),
            pltpu.VMEM((seq, 2 * tq), F32),
            pltpu.VMEM((seq, 2 * tq), F32),
            pltpu.VMEM((seq, 2 * tq), BF16),
            pltpu.VMEM((seq, 2 * tq), BF16),
        ],
        compiler_params=pltpu.CompilerParams(
            dimension_semantics=("arbitrary", "arbitrary"),
            vmem_limit_bytes=VMEM_LIMIT_BYTES),
        name="diff_attn" if bands is None else "diff_attn_banded",
    )(slopes, proj, proj, proj, lq1, lk1, lq2, lk2, subln_g)


def _attention(proj, slopes, lq1, lk1, lq2, lk2, subln_g, lambda_init):
    args = (proj, slopes, lq1, lk1, lq2, lk2, subln_g, lambda_init)
    n_band = len(ATTN_BANDS)
    rest = _diff_attn(*args, n_band, N_HEADS - n_band)
    near, ok = _diff_attn(*args, 0, n_band, ATTN_BANDS)
    near = lax.cond(jnp.all(ok > 0.5), lambda: near, lambda: _diff_attn(*args, 0, n_band))
    return near, rest


def _epilogue_kernel(alpha, on_a_ref, on_b_ref, zs_ref, ga_ref, gb_ref, ya_ref, x_ref, p_ref,
                     wpb_ref, wout_ref, wgate_ref, wple_ref, lng_ref, lnb_ref, out_ref):
    rows = on_a_ref.shape[0] // EPI_SPLIT
    for h in range(EPI_SPLIT):
        r = slice(h * rows, (h + 1) * rows)
        ob = jnp.concatenate([on_a_ref[r, :], on_b_ref[r, :]], axis=1) * zs_ref[r, :]
        yb = _dot(ob, wpb_ref[...])
        merged = (ga_ref[r, :].astype(F32) * ya_ref[r, :].astype(F32)
                  + gb_ref[r, :].astype(F32) * yb)
        res = alpha * x_ref[r, :] + _dot(merged.astype(BF16), wout_ref[...])
        gate = _sigmoid(_dot(res.astype(BF16), wgate_ref[...]))
        res = res + gate * _dot(p_ref[r, :].astype(BF16), wple_ref[...])
        mu = jnp.mean(res, axis=-1, keepdims=True)
        rc = res - mu
        var = jnp.mean(rc * rc, axis=-1, keepdims=True)
        out_ref[r, :] = rc * lax.rsqrt(var + LN_EPS) * lng_ref[...] + lnb_ref[...]


def _epilogue(on_a, on_b, proj, ya, x, p, w_proj_b, w_out, w_gate, w_ple, ln_g, ln_b, alpha):
    n, d = x.shape
    ple = p.shape[1]
    t = min(EPI_T, n)
    nt = n // t
    row = lambda w: pl.BlockSpec((t, w), lambda i: (i, 0))
    full = lambda r, c: pl.BlockSpec((r, c), lambda i: (0, 0))
    return pl.pallas_call(
        functools.partial(_epilogue_kernel, alpha),
        grid=(nt,),
        in_specs=[
            row(on_a.shape[1]), row(on_b.shape[1]),
            pl.BlockSpec((None, t, d), lambda i: (3, i, 0)),
            pl.BlockSpec((None, t, d), lambda i: (4, i, 0)),
            pl.BlockSpec((None, t, d), lambda i: (5, i, 0)),
            row(d), row(d), row(ple),
            full(d, d), full(d, d), full(d, d), full(ple, d), full(1, d), full(1, d),
        ],
        out_specs=row(d),
        out_shape=jax.ShapeDtypeStruct((n, d), F32),
        compiler_params=pltpu.CompilerParams(
            dimension_semantics=("arbitrary",), vmem_limit_bytes=VMEM_LIMIT_BYTES),
        name="epilogue",
    )(on_a, on_b, proj, proj, proj, ya, x, p, w_proj_b, w_out, w_gate, w_ple, ln_g, ln_b)


def kernel(x, p, w_in, conv_w, conv_b, w_proj_a, lambda_q1, lambda_k1, lambda_q2, lambda_k2,
           subln_g, w_proj_b, w_out, w_ple, w_ple_gate, ln_g, ln_b):
    depth = w_in.shape[0]
    bsz, seq, d = x.shape
    alpha = (2.0 * depth) ** 0.25
    slopes = jnp.asarray(
        np.float32(2.0) ** (-8.0 * np.arange(1, N_HEADS + 1, dtype=np.float32) / N_HEADS))
    h = x
    for i in range(depth):
        lambda_init = 0.8 - 0.6 * math.exp(-0.3 * i)
        w_in_b = w_in[i].astype(BF16)
        proj = _proj(h.reshape(bsz * seq, d), w_in_b).reshape(N_PROJ + 1, bsz, seq, d)
        ya = _conv_branch(proj, w_in_b, conv_w[i], conv_b[i][None, :], w_proj_a[i].astype(BF16))
        on_a, on_b = _attention(proj, slopes, lambda_q1[i][None, :], lambda_k1[i][None, :],
                                lambda_q2[i][None, :], lambda_k2[i][None, :], subln_g[i][:, None],
                                lambda_init)
        out = _epilogue(on_a.reshape(bsz * seq, -1), on_b.reshape(bsz * seq, -1),
                        proj.reshape(N_PROJ + 1, bsz * seq, d),
                        ya.reshape(bsz * seq, d), h.reshape(bsz * seq, d),
                        p[i].reshape(bsz * seq, -1), w_proj_b[i].astype(BF16),
                        w_out[i].astype(BF16), w_ple_gate[i].astype(BF16),
                        w_ple[i].astype(BF16), ln_g[i][None, :], ln_b[i][None, :], alpha)
        h = out.reshape(bsz, seq, d)
    return h
```

```python
import functools
import math

import jax
import jax.numpy as jnp
import numpy as np
from jax import lax
from jax.experimental import pallas as pl
from jax.experimental.pallas import tpu as pltpu

N_HEADS = 8
HEAD_DIM = 64
V_DIM = 2 * HEAD_DIM
LN_EPS = 1e-5
RMS_EPS = 1e-5
CONV_K = 3
LOG2E = math.log2(math.e)

VMEM_LIMIT_BYTES = 56 * 1024 * 1024
SUBLANES = 8
LANES = 128
BF16_SUBLANES = 16

N_PROJ = 6
PROJ_T = 1024
CONV_CB = 256
ATTN_TQ = 256
ATTN_KC = 256
ATTN_BANDS = (1, 2, 4)
UNDERFLOW = 130.0
BOUND_SLACK = 1.02
EPI_T = 1024
EPI_SPLIT = 4

BF16 = jnp.bfloat16
F32 = jnp.float32


def _dot(a, b):
    return jnp.dot(a, b, preferred_element_type=F32)


def _sigmoid(x):
    return 0.5 * jnp.tanh(0.5 * x) + 0.5


def _silu(x):
    return x * _sigmoid(x)


def _proj_kernel(x_ref, wq_ref, wk_ref, wv_ref, wz_ref, wga_ref, wgb_ref, o_ref):
    xb = x_ref[...].astype(BF16)
    o_ref[0] = (_dot(xb, wq_ref[...]) * (HEAD_DIM ** -0.5 * LOG2E)).astype(BF16)
    o_ref[1] = _dot(xb, wk_ref[...]).astype(BF16)
    o_ref[2] = _dot(xb, wv_ref[...]).astype(BF16)
    o_ref[3] = _silu(_dot(xb, wz_ref[...])).astype(BF16)
    o_ref[4] = _sigmoid(_dot(xb, wga_ref[...])).astype(BF16)
    o_ref[5] = _sigmoid(_dot(xb, wgb_ref[...])).astype(BF16)
    o_ref[6] = xb


def _proj(x, w_in):
    n, d = x.shape
    t = min(PROJ_T, n)
    wspec = lambda g: pl.BlockSpec((d, d), lambda i, g=g: (0, 4 + g), pipeline_mode=pl.Buffered(1))
    return pl.pallas_call(
        _proj_kernel,
        grid=(n // t,),
        in_specs=[pl.BlockSpec((t, d), lambda i: (i, 0))] + [wspec(g) for g in range(N_PROJ)],
        out_specs=pl.BlockSpec((N_PROJ + 1, t, d), lambda i: (0, i, 0)),
        out_shape=jax.ShapeDtypeStruct((N_PROJ + 1, n, d), BF16),
        compiler_params=pltpu.CompilerParams(
            dimension_semantics=("arbitrary",), vmem_limit_bytes=VMEM_LIMIT_BYTES),
        name="proj",
    )(x, *([w_in] * N_PROJ))


def _conv_branch_kernel(x_ref, wu_ref, wc_ref, wb_ref, wz_ref, cw_ref, cb_ref, wpa_ref, ya_ref,
                        y_scr):
    seq = x_ref.shape[0]
    width = wpa_ref.shape[0]
    xb = x_ref[...]
    row = lax.broadcasted_iota(jnp.int32, (seq, CONV_CB), 0)
    for c0 in range(0, width, CONV_CB):
        cols = slice(c0, c0 + CONV_CB)
        h = _dot(xb, wc_ref[:, cols]) * _dot(xb, wu_ref[:, cols])
        h_prev = jnp.where(row == 0, 0.0, pltpu.roll(h, 1, axis=0))
        h_next = jnp.where(row == seq - 1, 0.0, pltpu.roll(h, seq - 1, axis=0))
        conv = (cw_ref[0:1, cols] * h_prev + cw_ref[1:2, cols] * h + cw_ref[2:3, cols] * h_next
                + cb_ref[:, cols])
        y = _dot(xb, wb_ref[:, cols]) * conv * _silu(_dot(xb, wz_ref[:, cols]))
        y_scr[:, cols] = y.astype(BF16)
    ya_ref[...] = _dot(y_scr[...], wpa_ref[...]).astype(BF16)


def _conv_branch(proj, w_in, conv_w, conv_b, w_proj_a):
    _, bsz, seq, d = proj.shape
    width = w_proj_a.shape[0]
    wspec = lambda g: pl.BlockSpec((d, width), lambda b, g=g: (0, g),
                                   pipeline_mode=pl.Buffered(1))
    const = lambda shape: pl.BlockSpec(shape, lambda b: (0, 0), pipeline_mode=pl.Buffered(1))
    return pl.pallas_call(
        _conv_branch_kernel,
        grid=(bsz,),
        in_specs=[
            pl.BlockSpec((None, None, seq, d), lambda b: (N_PROJ, b, 0, 0)),
            wspec(0), wspec(1), wspec(2), wspec(3),
            const((CONV_K, width)), const((1, width)), const((width, d)),
        ],
        out_specs=pl.BlockSpec((None, seq, d), lambda b: (b, 0, 0)),
        out_shape=jax.ShapeDtypeStruct((bsz, seq, d), BF16),
        scratch_shapes=[pltpu.VMEM((seq, width), BF16)],
        compiler_params=pltpu.CompilerParams(
            dimension_semantics=("arbitrary",), vmem_limit_bytes=VMEM_LIMIT_BYTES),
        name="conv_branch",
    )(proj, w_in, w_in, w_in, w_in, conv_w, conv_b, w_proj_a)


def _diff_attn_kernel(lambda_init, tq, head0, bands, slopes_ref, q_ref, k_ref, v_ref, lq1_ref,
                      lk1_ref, lq2_ref, lk2_ref, g_ref, o_ref, *rest):
    if bands is None:
        ok_ref = None
        bias_scr, vt_scr, qs_scr, s0_scr, s1_scr, p0_scr, p1_scr = rest
    else:
        ok_ref, bias_scr, vt_scr, qs_scr, s0_scr, s1_scr, p0_scr, p1_scr = rest
    s_scr = (s0_scr, s1_scr)
    p_scr = (p0_scr, p1_scr)
    head = head0 + pl.program_id(0)
    b = pl.program_id(1)
    seq = k_ref.shape[0]
    nq = seq // tq
    kc = min(ATTN_KC, seq)
    n_kc = seq // kc
    sub = SUBLANES

    @pl.when(b == 0)
    def _():
        r = lax.broadcasted_iota(jnp.int32, bias_scr.shape, 0)
        i = lax.broadcasted_iota(jnp.int32, bias_scr.shape, 1)
        dist = jnp.abs(r - (seq - tq) - i).astype(F32)
        bias_scr[...] = dist * (-LOG2E * slopes_ref[head])

        vt_scr[V_DIM:V_DIM + BF16_SUBLANES, :] = jnp.where(
            lax.broadcasted_iota(jnp.int32, (BF16_SUBLANES, seq), 0) == 0, 1.0, 0.0).astype(BF16)

    vt_scr[0:V_DIM, :] = v_ref[...].T

    lam = (jnp.exp(jnp.sum(lq1_ref[...] * lk1_ref[...], keepdims=True))
           - jnp.exp(jnp.sum(lq2_ref[...] * lk2_ref[...], keepdims=True))
           + lambda_init)

    lane = lax.broadcasted_iota(jnp.int32, (1, V_DIM), 1)
    map1_lanes = jnp.where(lane < HEAD_DIM, 1.0, 0.0).astype(BF16)
    map2_lanes = jnp.where(lane >= HEAD_DIM, 1.0, 0.0).astype(BF16)

    def stage_queries(t):
        q = q_ref[t * tq:(t + 1) * tq, :]
        qs_scr[...] = jnp.concatenate([q * map1_lanes, q * map2_lanes], axis=0)

    def phase_a_chunk(t, i):
        r0 = i * kc
        s = lax.dot_general(k_ref[r0:r0 + kc, :], qs_scr[...], (((1,), (1,)), ((), ())),
                            preferred_element_type=F32)
        b0 = (seq - tq) - t * tq + r0
        bias = bias_scr[b0:b0 + kc, :]
        s_scr[t % 2][r0:r0 + kc, :] = s + jnp.concatenate([bias, bias], axis=1)

    def logits_chunk(t, i):
        return s_scr[t % 2][i * kc:(i + 1) * kc, :].reshape(kc // sub, sub, 2 * tq)

    def column_max(t, chunks):
        m8 = jnp.full((sub, 2 * tq), -jnp.inf, F32)
        for i in chunks:
            m8 = jnp.maximum(m8, jnp.max(logits_chunk(t, i), axis=0))
        return jnp.broadcast_to(jnp.max(m8, axis=0, keepdims=True), m8.shape)

    def phase_b_chunk(t, i, m8):
        r0 = i * kc
        p = jnp.exp2(logits_chunk(t, i) - m8[None])
        p_scr[t % 2][r0:r0 + kc, :] = p.reshape(kc, 2 * tq).astype(BF16)

    def finalize(t, chunks):
        rows = slice(chunks[0] * kc, (chunks[-1] + 1) * kc)
        ot = _dot(vt_scr[:, rows], p_scr[t % 2][rows, :])
        ot = ot[0:V_DIM, :] / ot[V_DIM:V_DIM + 1, :]
        odt = ot[:, 0:tq] - lam * ot[:, tq:2 * tq]
        ms = jnp.mean(odt * odt, axis=0, keepdims=True)
        ont = odt * lax.rsqrt(ms + RMS_EPS) * g_ref[...] * (1.0 - lambda_init)
        o_ref[t * tq:(t + 1) * tq, :] = ont.T.astype(BF16)

    def pipeline(band):
        def chunks_of(t):
            return [i for i in range(n_kc) if band is None or abs(i * kc // tq - t) <= band]

        for t in range(-1, nq + 1):
            if 0 <= t - 1 < nq:
                finalize(t - 1, chunks_of(t - 1))
            if t + 1 < nq:
                stage_queries(t + 1)
            if 0 <= t < nq:
                m8 = column_max(t, chunks_of(t))
            for i in range(n_kc):
                if t + 1 < nq and i in chunks_of(t + 1):
                    phase_a_chunk(t + 1, i)
                if 0 <= t < nq and i in chunks_of(t):
                    phase_b_chunk(t, i, m8)

    if bands is None:
        pipeline(None)
        return

    qf = q_ref[...].astype(F32)
    kf = k_ref[...].astype(F32)
    terms = jnp.concatenate([(qf * qf).astype(BF16), (kf * kf).astype(BF16),
                             (qf * kf).astype(BF16)], axis=1)
    r = lax.broadcasted_iota(jnp.int32, (3 * V_DIM, V_DIM), 0)
    c = lax.broadcasted_iota(jnp.int32, (3 * V_DIM, V_DIM), 1)
    selector = jnp.where(c == r // HEAD_DIM, 1.0, 0.0).astype(BF16)
    sums = _dot(terms, selector)
    hi = jnp.max(sums, axis=0, keepdims=True)
    lo = jnp.min(sums, axis=0, keepdims=True)
    lane_s = lax.broadcasted_iota(jnp.int32, (1, V_DIM), 1)
    pick = lambda v, col: jnp.sum(jnp.where(lane_s == col, v, 0.0), axis=1, keepdims=True)
    slope = LOG2E * slopes_ref[head]
    margin = jnp.maximum(
        BOUND_SLACK * jnp.sqrt(pick(hi, 0) * pick(hi, 2)) - pick(lo, 4),
        BOUND_SLACK * jnp.sqrt(pick(hi, 1) * pick(hi, 3)) - pick(lo, 5))
    for w, band in enumerate(bands):
        @pl.when(pl.program_id(0) == w)
        def _(band=band):
            reach = slope * (band * tq + 1)
            good = margin + UNDERFLOW <= reach
            ok_ref[...] = jnp.broadcast_to(jnp.where(good, 1.0, 0.0), ok_ref.shape)
            pipeline(band)


def _diff_attn(proj, slopes, lq1, lk1, lq2, lk2, subln_g, lambda_init, head0, n_heads, bands=None):
    _, bsz, seq, d = proj.shape
    tq = min(ATTN_TQ, seq)
    vec = lambda n: pl.BlockSpec((1, n), lambda h, b: (0, 0))
    out_specs = pl.BlockSpec((None, seq, V_DIM), lambda h, b: (b, 0, h))
    out_shape = jax.ShapeDtypeStruct((bsz, seq, n_heads * V_DIM), BF16)
    if bands is not None:
        assert len(bands) == n_heads
        out_specs = (out_specs,
                     pl.BlockSpec((None, None, SUBLANES, LANES), lambda h, b: (h, b, 0, 0)))
        out_shape = (out_shape, jax.ShapeDtypeStruct((n_heads, bsz, SUBLANES, LANES), F32))
    return pl.pallas_call(
        functools.partial(_diff_attn_kernel, lambda_init, tq, head0, bands),
        grid=(n_heads, bsz),
        in_specs=[
            pl.BlockSpec(memory_space=pltpu.SMEM),
            pl.BlockSpec((None, None, seq, V_DIM), lambda h, b: (0, b, 0, head0 + h)),
            pl.BlockSpec((None, None, seq, V_DIM), lambda h, b: (1, b, 0, head0 + h)),
            pl.BlockSpec((None, None, seq, V_DIM), lambda h, b: (2, b, 0, head0 + h)),
            vec(HEAD_DIM), vec(HEAD_DIM), vec(HEAD_DIM), vec(HEAD_DIM),
            pl.BlockSpec((V_DIM, 1), lambda h, b: (0, 0)),
        ],
        out_specs=out_specs,
        out_shape=out_shape,
        scratch_shapes=[
            pltpu.VMEM((2 * seq - tq, tq), F32),
            pltpu.VMEM((V_DIM + BF16_SUBLANES, seq), BF16),
            pltpu.VMEM((2 * tq, V_DIM), BF16),
            pltpu.VMEM((seq, 2 * tq), F32),
            pltpu.VMEM((seq, 2 * tq), F32),
            pltpu.VMEM((seq, 2 * tq), BF16),
            pltpu.VMEM((seq, 2 * tq), BF16),
        ],
        compiler_params=pltpu.CompilerParams(
            dimension_semantics=("arbitrary", "arbitrary"),
            vmem_limit_bytes=VMEM_LIMIT_BYTES),
        name="diff_attn" if bands is None else "diff_attn_banded",
    )(slopes, proj, proj, proj, lq1, lk1, lq2, lk2, subln_g)


def _attention(proj, slopes, lq1, lk1, lq2, lk2, subln_g, lambda_init):
    args = (proj, slopes, lq1, lk1, lq2, lk2, subln_g, lambda_init)
    n_band = len(ATTN_BANDS)
    rest = _diff_attn(*args, n_band, N_HEADS - n_band)
    near, ok = _diff_attn(*args, 0, n_band, ATTN_BANDS)
    near = lax.cond(jnp.all(ok > 0.5), lambda: near, lambda: _diff_attn(*args, 0, n_band))
    return near, rest


def _epilogue_kernel(alpha, on_a_ref, on_b_ref, zs_ref, ga_ref, gb_ref, ya_ref, x_ref, p_ref,
                     wpb_ref, wout_ref, wgate_ref, wple_ref, lng_ref, lnb_ref, out_ref):
    rows = on_a_ref.shape[0] // EPI_SPLIT
    rs = [slice(h * rows, (h + 1) * rows) for h in range(EPI_SPLIT)]
    yb = [_dot(jnp.concatenate([on_a_ref[r, :], on_b_ref[r, :]], axis=1) * zs_ref[r, :],
               wpb_ref[...]) for r in rs]
    pw = [_dot(p_ref[r, :].astype(BF16), wple_ref[...]) for r in rs]
    merged = [(ga_ref[r, :].astype(F32) * ya_ref[r, :].astype(F32)
               + gb_ref[r, :].astype(F32) * y).astype(BF16) for r, y in zip(rs, yb)]
    res = [alpha * x_ref[r, :] + _dot(m, wout_ref[...]) for r, m in zip(rs, merged)]
    gate = [_sigmoid(_dot(v.astype(BF16), wgate_ref[...])) for v in res]
    for r, v, g, w in zip(rs, res, gate, pw):
        v = v + g * w
        mu = jnp.mean(v, axis=-1, keepdims=True)
        rc = v - mu
        var = jnp.mean(rc * rc, axis=-1, keepdims=True)
        out_ref[r, :] = rc * lax.rsqrt(var + LN_EPS) * lng_ref[...] + lnb_ref[...]


def _epilogue(on_a, on_b, proj, ya, x, p, w_proj_b, w_out, w_gate, w_ple, ln_g, ln_b, alpha):
    n, d = x.shape
    ple = p.shape[1]
    t = min(EPI_T, n)
    nt = n // t
    row = lambda w: pl.BlockSpec((t, w), lambda i: (i, 0))
    full = lambda r, c: pl.BlockSpec((r, c), lambda i: (0, 0))
    return pl.pallas_call(
        functools.partial(_epilogue_kernel, alpha),
        grid=(nt,),
        in_specs=[
            row(on_a.shape[1]), row(on_b.shape[1]),
            pl.BlockSpec((None, t, d), lambda i: (3, i, 0)),
            pl.BlockSpec((None, t, d), lambda i: (4, i, 0)),
            pl.BlockSpec((None, t, d), lambda i: (5, i, 0)),
            row(d), row(d), row(ple),
            full(d, d), full(d, d), full(d, d), full(ple, d), full(1, d), full(1, d),
        ],
        out_specs=row(d),
        out_shape=jax.ShapeDtypeStruct((n, d), F32),
        compiler_params=pltpu.CompilerParams(
            dimension_semantics=("arbitrary",), vmem_limit_bytes=VMEM_LIMIT_BYTES),
        name="epilogue",
    )(on_a, on_b, proj, proj, proj, ya, x, p, w_proj_b, w_out, w_gate, w_ple, ln_g, ln_b)


def kernel(x, p, w_in, conv_w, conv_b, w_proj_a, lambda_q1, lambda_k1, lambda_q2, lambda_k2,
           subln_g, w_proj_b, w_out, w_ple, w_ple_gate, ln_g, ln_b):
    depth = w_in.shape[0]
    bsz, seq, d = x.shape
    alpha = (2.0 * depth) ** 0.25
    slopes = jnp.asarray(
        np.float32(2.0) ** (-8.0 * np.arange(1, N_HEADS + 1, dtype=np.float32) / N_HEADS))
    h = x
    for i in range(depth):
        lambda_init = 0.8 - 0.6 * math.exp(-0.3 * i)
        w_in_b = w_in[i].astype(BF16)
        proj = _proj(h.reshape(bsz * seq, d), w_in_b).reshape(N_PROJ + 1, bsz, seq, d)
        ya = _conv_branch(proj, w_in_b, conv_w[i], conv_b[i][None, :], w_proj_a[i].astype(BF16))
        on_a, on_b = _attention(proj, slopes, lambda_q1[i][None, :], lambda_k1[i][None, :],
                                lambda_q2[i][None, :], lambda_k2[i][None, :], subln_g[i][:, None],
                                lambda_init)
        out = _epilogue(on_a.reshape(bsz * seq, -1), on_b.reshape(bsz * seq, -1),
                        proj.reshape(N_PROJ + 1, bsz * seq, d),
                        ya.reshape(bsz * seq, d), h.reshape(bsz * seq, d),
                        p[i].reshape(bsz * seq, -1), w_proj_b[i].astype(BF16),
                        w_out[i].astype(BF16), w_ple_gate[i].astype(BF16),
                        w_ple[i].astype(BF16), ln_g[i][None, :], ln_b[i][None, :], alpha)
        h = out.reshape(bsz, seq, d)
    return h
```

```python
import functools
import math

import jax
import jax.numpy as jnp
import numpy as np
from jax import lax
from jax.experimental import pallas as pl
from jax.experimental.pallas import tpu as pltpu

N_HEADS = 8
HEAD_DIM = 64
V_DIM = 2 * HEAD_DIM
LN_EPS = 1e-5
RMS_EPS = 1e-5
CONV_K = 3
LOG2E = math.log2(math.e)

VMEM_LIMIT_BYTES = 56 * 1024 * 1024
SUBLANES = 8
LANES = 128
BF16_SUBLANES = 16

N_PROJ = 6
PROJ_T = 1024
CONV_CB = 256
ATTN_TQ = 256
ATTN_KC = 256
ATTN_BANDS = (1, 2, 4)
UNDERFLOW = 130.0
BOUND_SLACK = 1.02
EXP_RANGE = 60.0
EPI_T = 1024
EPI_SPLIT = 4

BF16 = jnp.bfloat16
F32 = jnp.float32


def _dot(a, b):
    return jnp.dot(a, b, preferred_element_type=F32)


def _sigmoid(x):
    return 0.5 * jnp.tanh(0.5 * x) + 0.5


def _silu(x):
    return x * _sigmoid(x)


def _proj_kernel(x_ref, wq_ref, wk_ref, wv_ref, wz_ref, wga_ref, wgb_ref, o_ref):
    xb = x_ref[...].astype(BF16)
    o_ref[0] = (_dot(xb, wq_ref[...]) * (HEAD_DIM ** -0.5 * LOG2E)).astype(BF16)
    o_ref[1] = _dot(xb, wk_ref[...]).astype(BF16)
    o_ref[2] = _dot(xb, wv_ref[...]).astype(BF16)
    o_ref[3] = _silu(_dot(xb, wz_ref[...])).astype(BF16)
    o_ref[4] = _sigmoid(_dot(xb, wga_ref[...])).astype(BF16)
    o_ref[5] = _sigmoid(_dot(xb, wgb_ref[...])).astype(BF16)
    o_ref[6] = xb


def _proj(x, w_in):
    n, d = x.shape
    t = min(PROJ_T, n)
    wspec = lambda g: pl.BlockSpec((d, d), lambda i, g=g: (0, 4 + g), pipeline_mode=pl.Buffered(1))
    return pl.pallas_call(
        _proj_kernel,
        grid=(n // t,),
        in_specs=[pl.BlockSpec((t, d), lambda i: (i, 0))] + [wspec(g) for g in range(N_PROJ)],
        out_specs=pl.BlockSpec((N_PROJ + 1, t, d), lambda i: (0, i, 0)),
        out_shape=jax.ShapeDtypeStruct((N_PROJ + 1, n, d), BF16),
        compiler_params=pltpu.CompilerParams(
            dimension_semantics=("arbitrary",), vmem_limit_bytes=VMEM_LIMIT_BYTES),
        name="proj",
    )(x, *([w_in] * N_PROJ))


def _conv_branch_kernel(x_ref, wu_ref, wc_ref, wb_ref, wz_ref, cw_ref, cb_ref, wpa_ref, ya_ref,
                        y_scr):
    seq = x_ref.shape[0]
    width = wpa_ref.shape[0]
    xb = x_ref[...]
    row = lax.broadcasted_iota(jnp.int32, (seq, CONV_CB), 0)
    for c0 in range(0, width, CONV_CB):
        cols = slice(c0, c0 + CONV_CB)
        h = _dot(xb, wc_ref[:, cols]) * _dot(xb, wu_ref[:, cols])
        h_prev = jnp.where(row == 0, 0.0, pltpu.roll(h, 1, axis=0))
        h_next = jnp.where(row == seq - 1, 0.0, pltpu.roll(h, seq - 1, axis=0))
        conv = (cw_ref[0:1, cols] * h_prev + cw_ref[1:2, cols] * h + cw_ref[2:3, cols] * h_next
                + cb_ref[:, cols])
        y = _dot(xb, wb_ref[:, cols]) * conv * _silu(_dot(xb, wz_ref[:, cols]))
        y_scr[:, cols] = y.astype(BF16)
    ya_ref[...] = _dot(y_scr[...], wpa_ref[...]).astype(BF16)


def _conv_branch(proj, w_in, conv_w, conv_b, w_proj_a):
    _, bsz, seq, d = proj.shape
    width = w_proj_a.shape[0]
    wspec = lambda g: pl.BlockSpec((d, width), lambda b, g=g: (0, g),
                                   pipeline_mode=pl.Buffered(1))
    const = lambda shape: pl.BlockSpec(shape, lambda b: (0, 0), pipeline_mode=pl.Buffered(1))
    return pl.pallas_call(
        _conv_branch_kernel,
        grid=(bsz,),
        in_specs=[
            pl.BlockSpec((None, None, seq, d), lambda b: (N_PROJ, b, 0, 0)),
            wspec(0), wspec(1), wspec(2), wspec(3),
            const((CONV_K, width)), const((1, width)), const((width, d)),
        ],
        out_specs=pl.BlockSpec((None, seq, d), lambda b: (b, 0, 0)),
        out_shape=jax.ShapeDtypeStruct((bsz, seq, d), BF16),
        scratch_shapes=[pltpu.VMEM((seq, width), BF16)],
        compiler_params=pltpu.CompilerParams(
            dimension_semantics=("arbitrary",), vmem_limit_bytes=VMEM_LIMIT_BYTES),
        name="conv_branch",
    )(proj, w_in, w_in, w_in, w_in, conv_w, conv_b, w_proj_a)


def _diff_attn_kernel(lambda_init, tq, head0, bands, one_pass, slopes_ref, q_ref, k_ref, v_ref,
                      lq1_ref, lk1_ref, lq2_ref, lk2_ref, g_ref, o_ref, *rest):
    if one_pass:
        ok_ref, bias_scr, vt_scr, qs_scr, s0_scr, s1_scr, p0_scr, p1_scr = rest
    else:
        ok_ref = None
        bias_scr, vt_scr, qs_scr, s0_scr, s1_scr, p0_scr, p1_scr = rest
    s_scr = (s0_scr, s1_scr)
    p_scr = (p0_scr, p1_scr)
    head = head0 + pl.program_id(0)
    b = pl.program_id(1)
    seq = k_ref.shape[0]
    nq = seq // tq
    kc = min(ATTN_KC, seq)
    n_kc = seq // kc
    sub = SUBLANES

    @pl.when(b == 0)
    def _():
        r = lax.broadcasted_iota(jnp.int32, bias_scr.shape, 0)
        i = lax.broadcasted_iota(jnp.int32, bias_scr.shape, 1)
        dist = jnp.abs(r - (seq - tq) - i).astype(F32)
        bias_scr[...] = dist * (-LOG2E * slopes_ref[head])

        vt_scr[V_DIM:V_DIM + BF16_SUBLANES, :] = jnp.where(
            lax.broadcasted_iota(jnp.int32, (BF16_SUBLANES, seq), 0) == 0, 1.0, 0.0).astype(BF16)

    vt_scr[0:V_DIM, :] = v_ref[...].T

    lam = (jnp.exp(jnp.sum(lq1_ref[...] * lk1_ref[...], keepdims=True))
           - jnp.exp(jnp.sum(lq2_ref[...] * lk2_ref[...], keepdims=True))
           + lambda_init)

    lane = lax.broadcasted_iota(jnp.int32, (1, V_DIM), 1)
    map1_lanes = jnp.where(lane < HEAD_DIM, 1.0, 0.0).astype(BF16)
    map2_lanes = jnp.where(lane >= HEAD_DIM, 1.0, 0.0).astype(BF16)

    def stage_queries(t):
        q = q_ref[t * tq:(t + 1) * tq, :]
        qs_scr[...] = jnp.concatenate([q * map1_lanes, q * map2_lanes], axis=0)

    def phase_a_chunk(t, i):
        r0 = i * kc
        s = lax.dot_general(k_ref[r0:r0 + kc, :], qs_scr[...], (((1,), (1,)), ((), ())),
                            preferred_element_type=F32)
        b0 = (seq - tq) - t * tq + r0
        bias = bias_scr[b0:b0 + kc, :]
        s_scr[t % 2][r0:r0 + kc, :] = s + jnp.concatenate([bias, bias], axis=1)

    def logits_chunk(t, i):
        return s_scr[t % 2][i * kc:(i + 1) * kc, :].reshape(kc // sub, sub, 2 * tq)

    def column_max(t, chunks):
        m8 = jnp.full((sub, 2 * tq), -jnp.inf, F32)
        for i in chunks:
            m8 = jnp.maximum(m8, jnp.max(logits_chunk(t, i), axis=0))
        return jnp.broadcast_to(jnp.max(m8, axis=0, keepdims=True), m8.shape)

    def phase_b_chunk(t, i, m8):
        r0 = i * kc
        p = jnp.exp2(logits_chunk(t, i) - m8[None])
        p_scr[t % 2][r0:r0 + kc, :] = p.reshape(kc, 2 * tq).astype(BF16)

    def one_pass_chunk(t, i):
        r0 = i * kc
        s = lax.dot_general(k_ref[r0:r0 + kc, :], qs_scr[...], (((1,), (1,)), ((), ())),
                            preferred_element_type=F32)
        b0 = (seq - tq) - t * tq + r0
        bias = bias_scr[b0:b0 + kc, :]
        p = jnp.exp2(s + jnp.concatenate([bias, bias], axis=1))
        p_scr[t % 2][r0:r0 + kc, :] = p.astype(BF16)

    def finalize(t, chunks):
        rows = slice(chunks[0] * kc, (chunks[-1] + 1) * kc)
        ot = _dot(vt_scr[:, rows], p_scr[t % 2][rows, :])
        ot = ot[0:V_DIM, :] / ot[V_DIM:V_DIM + 1, :]
        odt = ot[:, 0:tq] - lam * ot[:, tq:2 * tq]
        ms = jnp.mean(odt * odt, axis=0, keepdims=True)
        ont = odt * lax.rsqrt(ms + RMS_EPS) * g_ref[...] * (1.0 - lambda_init)
        o_ref[t * tq:(t + 1) * tq, :] = ont.T.astype(BF16)

    def chunks_of(t, band):
        return [i for i in range(n_kc) if band is None or abs(i * kc // tq - t) <= band]

    def two_pass_pipeline():
        every = chunks_of(0, None)
        for t in range(-1, nq + 1):
            if 0 <= t - 1 < nq:
                finalize(t - 1, every)
            if t + 1 < nq:
                stage_queries(t + 1)
            if 0 <= t < nq:
                m8 = column_max(t, every)
            for i in every:
                if t + 1 < nq:
                    phase_a_chunk(t + 1, i)
                if 0 <= t < nq:
                    phase_b_chunk(t, i, m8)

    def one_pass_pipeline(band):
        for t in range(0, nq + 1):
            if 0 <= t - 1 < nq:
                finalize(t - 1, chunks_of(t - 1, band))
            if t < nq:
                stage_queries(t)
                for i in chunks_of(t, band):
                    one_pass_chunk(t, i)

    if not one_pass:
        two_pass_pipeline()
        return

    qf = q_ref[...].astype(F32)
    kf = k_ref[...].astype(F32)
    terms = jnp.concatenate([(qf * qf).astype(BF16), (kf * kf).astype(BF16),
                             (qf * kf).astype(BF16)], axis=1)
    r = lax.broadcasted_iota(jnp.int32, (3 * V_DIM, V_DIM), 0)
    c = lax.broadcasted_iota(jnp.int32, (3 * V_DIM, V_DIM), 1)
    selector = jnp.where(c == r // HEAD_DIM, 1.0, 0.0).astype(BF16)
    sums = _dot(terms, selector)
    top = jnp.max(sums, axis=0, keepdims=True)
    bot = jnp.min(sums, axis=0, keepdims=True)
    lane_s = lax.broadcasted_iota(jnp.int32, (1, V_DIM), 1)
    pick = lambda v, col: jnp.sum(jnp.where(lane_s == col, v, 0.0), axis=1, keepdims=True)
    hi = BOUND_SLACK * jnp.sqrt(jnp.maximum(pick(top, 0) * pick(top, 2),
                                            pick(top, 1) * pick(top, 3)))
    lo = jnp.minimum(pick(bot, 4), pick(bot, 5)) - (BOUND_SLACK - 1.0) * hi
    in_range = jnp.logical_and(hi <= EXP_RANGE, lo >= -EXP_RANGE)
    slope = LOG2E * slopes_ref[head]

    def report(good):
        ok_ref[...] = jnp.broadcast_to(jnp.where(good, 1.0, 0.0), ok_ref.shape)

    if bands is None:
        report(in_range)
        one_pass_pipeline(None)
        return
    for w, band in enumerate(bands):
        @pl.when(pl.program_id(0) == w)
        def _(band=band):
            reach = slope * (band * tq + 1)
            report(jnp.logical_and(in_range, hi - lo + UNDERFLOW <= reach))
            one_pass_pipeline(band)


def _diff_attn(proj, slopes, lq1, lk1, lq2, lk2, subln_g, lambda_init, head0, n_heads,
               one_pass=False, bands=None):
    _, bsz, seq, d = proj.shape
    tq = min(ATTN_TQ, seq)
    vec = lambda n: pl.BlockSpec((1, n), lambda h, b: (0, 0))
    out_specs = pl.BlockSpec((None, seq, V_DIM), lambda h, b: (b, 0, h))
    out_shape = jax.ShapeDtypeStruct((bsz, seq, n_heads * V_DIM), BF16)
    assert bands is None or (one_pass and len(bands) == n_heads)
    if one_pass:
        out_specs = (out_specs,
                     pl.BlockSpec((None, None, SUBLANES, LANES), lambda h, b: (h, b, 0, 0)))
        out_shape = (out_shape, jax.ShapeDtypeStruct((n_heads, bsz, SUBLANES, LANES), F32))
    return pl.pallas_call(
        functools.partial(_diff_attn_kernel, lambda_init, tq, head0, bands, one_pass),
        grid=(n_heads, bsz),
        in_specs=[
            pl.BlockSpec(memory_space=pltpu.SMEM),
            pl.BlockSpec((None, None, seq, V_DIM), lambda h, b: (0, b, 0, head0 + h)),
            pl.BlockSpec((None, None, seq, V_DIM), lambda h, b: (1, b, 0, head0 + h)),
            pl.BlockSpec((None, None, seq, V_DIM), lambda h, b: (2, b, 0, head0 + h)),
            vec(HEAD_DIM), vec(HEAD_DIM), vec(HEAD_DIM), vec(HEAD_DIM),
            pl.BlockSpec((V_DIM, 1), lambda h, b: (0, 0)),
        ],
        out_specs=out_specs,
        out_shape=out_shape,
        scratch_shapes=[
            pltpu.VMEM((2 * seq - tq, tq), F32),
            pltpu.VMEM((V_DIM + BF16_SUBLANES, seq), BF16),
            pltpu.VMEM((2 * tq, V_DIM), BF16),
            pltpu.VMEM((seq, 2 * tq), F32),
            pltpu.VMEM((seq, 2 * tq), F32),
            pltpu.VMEM((seq, 2 * tq), BF16),
            pltpu.VMEM((seq, 2 * tq), BF16),
        ],
        compiler_params=pltpu.CompilerParams(
            dimension_semantics=("arbitrary", "arbitrary"),
            vmem_limit_bytes=VMEM_LIMIT_BYTES),
        name=("diff_attn_two_pass" if not one_pass else
              "diff_attn" if bands is None else "diff_attn_banded"),
    )(slopes, proj, proj, proj, lq1, lk1, lq2, lk2, subln_g)


def _attention(proj, slopes, lq1, lk1, lq2, lk2, subln_g, lambda_init):
    args = (proj, slopes, lq1, lk1, lq2, lk2, subln_g, lambda_init)
    n_band = len(ATTN_BANDS)

    def heads(head0, n_heads, bands):
        fast, ok = _diff_attn(*args, head0, n_heads, one_pass=True, bands=bands)
        return lax.cond(jnp.all(ok > 0.5), lambda: fast,
                        lambda: _diff_attn(*args, head0, n_heads))

    return heads(0, n_band, ATTN_BANDS), heads(n_band, N_HEADS - n_band, None)


def _epilogue_kernel(alpha, on_a_ref, on_b_ref, zs_ref, ga_ref, gb_ref, ya_ref, x_ref, p_ref,
                     wpb_ref, wout_ref, wgate_ref, wple_ref, lng_ref, lnb_ref, out_ref):
    rows = on_a_ref.shape[0] // EPI_SPLIT
    rs = [slice(h * rows, (h + 1) * rows) for h in range(EPI_SPLIT)]
    yb = [_dot(jnp.concatenate([on_a_ref[r, :], on_b_ref[r, :]], axis=1) * zs_ref[r, :],
               wpb_ref[...]) for r in rs]
    pw = [_dot(p_ref[r, :].astype(BF16), wple_ref[...]) for r in rs]
    merged = [(ga_ref[r, :].astype(F32) * ya_ref[r, :].astype(F32)
               + gb_ref[r, :].astype(F32) * y).astype(BF16) for r, y in zip(rs, yb)]
    res = [alpha * x_ref[r, :] + _dot(m, wout_ref[...]) for r, m in zip(rs, merged)]
    gate = [_sigmoid(_dot(v.astype(BF16), wgate_ref[...])) for v in res]
    for r, v, g, w in zip(rs, res, gate, pw):
        v = v + g * w
        mu = jnp.mean(v, axis=-1, keepdims=True)
        rc = v - mu
        var = jnp.mean(rc * rc, axis=-1, keepdims=True)
        out_ref[r, :] = rc * lax.rsqrt(var + LN_EPS) * lng_ref[...] + lnb_ref[...]


def _epilogue(on_a, on_b, proj, ya, x, p, w_proj_b, w_out, w_gate, w_ple, ln_g, ln_b, alpha):
    n, d = x.shape
    ple = p.shape[1]
    t = min(EPI_T, n)
    nt = n // t
    row = lambda w: pl.BlockSpec((t, w), lambda i: (i, 0))
    full = lambda r, c: pl.BlockSpec((r, c), lambda i: (0, 0))
    return pl.pallas_call(
        functools.partial(_epilogue_kernel, alpha),
        grid=(nt,),
        in_specs=[
            row(on_a.shape[1]), row(on_b.shape[1]),
            pl.BlockSpec((None, t, d), lambda i: (3, i, 0)),
            pl.BlockSpec((None, t, d), lambda i: (4, i, 0)),
            pl.BlockSpec((None, t, d), lambda i: (5, i, 0)),
            row(d), row(d), row(ple),
            full(d, d), full(d, d), full(d, d), full(ple, d), full(1, d), full(1, d),
        ],
        out_specs=row(d),
        out_shape=jax.ShapeDtypeStruct((n, d), F32),
        compiler_params=pltpu.CompilerParams(
            dimension_semantics=("arbitrary",), vmem_limit_bytes=VMEM_LIMIT_BYTES),
        name="epilogue",
    )(on_a, on_b, proj, proj, proj, ya, x, p, w_proj_b, w_out, w_gate, w_ple, ln_g, ln_b)


def kernel(x, p, w_in, conv_w, conv_b, w_proj_a, lambda_q1, lambda_k1, lambda_q2, lambda_k2,
           subln_g, w_proj_b, w_out, w_ple, w_ple_gate, ln_g, ln_b):
    depth = w_in.shape[0]
    bsz, seq, d = x.shape
    alpha = (2.0 * depth) ** 0.25
    slopes = jnp.asarray(
        np.float32(2.0) ** (-8.0 * np.arange(1, N_HEADS + 1, dtype=np.float32) / N_HEADS))
    h = x
    for i in range(depth):
        lambda_init = 0.8 - 0.6 * math.exp(-0.3 * i)
        w_in_b = w_in[i].astype(BF16)
        proj = _proj(h.reshape(bsz * seq, d), w_in_b).reshape(N_PROJ + 1, bsz, seq, d)
        ya = _conv_branch(proj, w_in_b, conv_w[i], conv_b[i][None, :], w_proj_a[i].astype(BF16))
        on_a, on_b = _attention(proj, slopes, lambda_q1[i][None, :], lambda_k1[i][None, :],
                                lambda_q2[i][None, :], lambda_k2[i][None, :], subln_g[i][:, None],
                                lambda_init)
        out = _epilogue(on_a.reshape(bsz * seq, -1), on_b.reshape(bsz * seq, -1),
                        proj.reshape(N_PROJ + 1, bsz * seq, d),
                        ya.reshape(bsz * seq, d), h.reshape(bsz * seq, d),
                        p[i].reshape(bsz * seq, -1), w_proj_b[i].astype(BF16),
                        w_out[i].astype(BF16), w_ple_gate[i].astype(BF16),
                        w_ple[i].astype(BF16), ln_g[i][None, :], ln_b[i][None, :], alpha)
        h = out.reshape(bsz, seq, d)
    return h
```

```python
import functools
import math

import jax
import jax.numpy as jnp
import numpy as np
from jax import lax
from jax.experimental import pallas as pl
from jax.experimental.pallas import tpu as pltpu

N_HEADS = 8
HEAD_DIM = 64
V_DIM = 2 * HEAD_DIM
LN_EPS = 1e-5
RMS_EPS = 1e-5
CONV_K = 3
LOG2E = math.log2(math.e)

VMEM_LIMIT_BYTES = 56 * 1024 * 1024
SUBLANES = 8
LANES = 128
BF16_SUBLANES = 16

N_PROJ = 6
PROJ_T = 1024
CONV_CB = 256
ATTN_TQ = 256
ATTN_KC = 256
ATTN_BANDS = (1, 2, 4)
UNDERFLOW = 130.0
BOUND_SLACK = 1.02
EXP_RANGE = 60.0
EPI_T = 1024
EPI_SPLIT = 4

BF16 = jnp.bfloat16
F32 = jnp.float32


def _dot(a, b):
    return jnp.dot(a, b, preferred_element_type=F32)


def _sigmoid(x):
    return 0.5 * jnp.tanh(0.5 * x) + 0.5


def _silu(x):
    return x * _sigmoid(x)


def _proj_kernel(x_ref, wq_ref, wk_ref, wv_ref, wz_ref, wga_ref, wgb_ref, o_ref):
    xb = x_ref[...].astype(BF16)
    o_ref[0] = (_dot(xb, wq_ref[...]) * (HEAD_DIM ** -0.5 * LOG2E)).astype(BF16)
    o_ref[1] = _dot(xb, wk_ref[...]).astype(BF16)
    o_ref[2] = _dot(xb, wv_ref[...]).astype(BF16)
    o_ref[3] = _silu(_dot(xb, wz_ref[...])).astype(BF16)
    o_ref[4] = _sigmoid(_dot(xb, wga_ref[...])).astype(BF16)
    o_ref[5] = _sigmoid(_dot(xb, wgb_ref[...])).astype(BF16)
    o_ref[6] = xb


def _proj(x, w_in):
    n, d = x.shape
    t = min(PROJ_T, n)
    wspec = lambda g: pl.BlockSpec((d, d), lambda i, g=g: (0, 4 + g), pipeline_mode=pl.Buffered(1))
    return pl.pallas_call(
        _proj_kernel,
        grid=(n // t,),
        in_specs=[pl.BlockSpec((t, d), lambda i: (i, 0))] + [wspec(g) for g in range(N_PROJ)],
        out_specs=pl.BlockSpec((N_PROJ + 1, t, d), lambda i: (0, i, 0)),
        out_shape=jax.ShapeDtypeStruct((N_PROJ + 1, n, d), BF16),
        compiler_params=pltpu.CompilerParams(
            dimension_semantics=("arbitrary",), vmem_limit_bytes=VMEM_LIMIT_BYTES),
        name="proj",
    )(x, *([w_in] * N_PROJ))


def _conv_branch_kernel(x_ref, wu_ref, wc_ref, wb_ref, wz_ref, cw_ref, cb_ref, wpa_ref, ya_ref,
                        y_scr):
    seq = x_ref.shape[0]
    width = wpa_ref.shape[0]
    xb = x_ref[...]
    row = lax.broadcasted_iota(jnp.int32, (seq, CONV_CB), 0)
    for c0 in range(0, width, CONV_CB):
        cols = slice(c0, c0 + CONV_CB)
        h = _dot(xb, wc_ref[:, cols]) * _dot(xb, wu_ref[:, cols])
        h_prev = jnp.where(row == 0, 0.0, pltpu.roll(h, 1, axis=0))
        h_next = jnp.where(row == seq - 1, 0.0, pltpu.roll(h, seq - 1, axis=0))
        conv = (cw_ref[0:1, cols] * h_prev + cw_ref[1:2, cols] * h + cw_ref[2:3, cols] * h_next
                + cb_ref[:, cols])
        y = _dot(xb, wb_ref[:, cols]) * conv * _silu(_dot(xb, wz_ref[:, cols]))
        y_scr[:, cols] = y.astype(BF16)
    ya_ref[...] = _dot(y_scr[...], wpa_ref[...]).astype(BF16)


def _conv_branch(proj, w_in, conv_w, conv_b, w_proj_a):
    _, bsz, seq, d = proj.shape
    width = w_proj_a.shape[0]
    wspec = lambda g: pl.BlockSpec((d, width), lambda b, g=g: (0, g),
                                   pipeline_mode=pl.Buffered(1))
    const = lambda shape: pl.BlockSpec(shape, lambda b: (0, 0), pipeline_mode=pl.Buffered(1))
    return pl.pallas_call(
        _conv_branch_kernel,
        grid=(bsz,),
        in_specs=[
            pl.BlockSpec((None, None, seq, d), lambda b: (N_PROJ, b, 0, 0)),
            wspec(0), wspec(1), wspec(2), wspec(3),
            const((CONV_K, width)), const((1, width)), const((width, d)),
        ],
        out_specs=pl.BlockSpec((None, seq, d), lambda b: (b, 0, 0)),
        out_shape=jax.ShapeDtypeStruct((bsz, seq, d), BF16),
        scratch_shapes=[pltpu.VMEM((seq, width), BF16)],
        compiler_params=pltpu.CompilerParams(
            dimension_semantics=("arbitrary",), vmem_limit_bytes=VMEM_LIMIT_BYTES),
        name="conv_branch",
    )(proj, w_in, w_in, w_in, w_in, conv_w, conv_b, w_proj_a)


def _diff_attn_kernel(lambda_init, tq, head0, bands, one_pass, slopes_ref, q_ref, k_ref, v_ref,
                      lq1_ref, lk1_ref, lq2_ref, lk2_ref, g_ref, o_ref, *rest):
    if one_pass:
        ok_ref, bias_scr, vt_scr, qs_scr, s0_scr, s1_scr, p0_scr, p1_scr = rest
    else:
        ok_ref = None
        bias_scr, vt_scr, qs_scr, s0_scr, s1_scr, p0_scr, p1_scr = rest
    s_scr = (s0_scr, s1_scr)
    p_scr = (p0_scr, p1_scr)
    head = head0 + pl.program_id(0)
    b = pl.program_id(1)
    seq = k_ref.shape[0]
    nq = seq // tq
    kc = min(ATTN_KC, seq)
    n_kc = seq // kc
    sub = SUBLANES

    @pl.when(b == 0)
    def _():
        r = lax.broadcasted_iota(jnp.int32, bias_scr.shape, 0)
        i = lax.broadcasted_iota(jnp.int32, bias_scr.shape, 1)
        dist = jnp.abs(r - (seq - tq) - i).astype(F32)
        bias_scr[...] = dist * (-LOG2E * slopes_ref[head])

        vt_scr[V_DIM:V_DIM + BF16_SUBLANES, :] = jnp.where(
            lax.broadcasted_iota(jnp.int32, (BF16_SUBLANES, seq), 0) == 0, 1.0, 0.0).astype(BF16)

    vt_scr[0:V_DIM, :] = v_ref[...].T

    lam = (jnp.exp(jnp.sum(lq1_ref[...] * lk1_ref[...], keepdims=True))
           - jnp.exp(jnp.sum(lq2_ref[...] * lk2_ref[...], keepdims=True))
           + lambda_init)

    lane = lax.broadcasted_iota(jnp.int32, (1, V_DIM), 1)
    map1_lanes = jnp.where(lane < HEAD_DIM, 1.0, 0.0).astype(BF16)
    map2_lanes = jnp.where(lane >= HEAD_DIM, 1.0, 0.0).astype(BF16)

    def stage_queries(t):
        q = q_ref[t * tq:(t + 1) * tq, :]
        qs_scr[...] = jnp.concatenate([q * map1_lanes, q * map2_lanes], axis=0)

    def phase_a_chunk(t, i):
        r0 = i * kc
        s = lax.dot_general(k_ref[r0:r0 + kc, :], qs_scr[...], (((1,), (1,)), ((), ())),
                            preferred_element_type=F32)
        b0 = (seq - tq) - t * tq + r0
        bias = bias_scr[b0:b0 + kc, :]
        s_scr[t % 2][r0:r0 + kc, :] = s + jnp.concatenate([bias, bias], axis=1)

    def logits_chunk(t, i):
        return s_scr[t % 2][i * kc:(i + 1) * kc, :].reshape(kc // sub, sub, 2 * tq)

    def column_max(t, chunks):
        m8 = jnp.full((sub, 2 * tq), -jnp.inf, F32)
        for i in chunks:
            m8 = jnp.maximum(m8, jnp.max(logits_chunk(t, i), axis=0))
        return jnp.broadcast_to(jnp.max(m8, axis=0, keepdims=True), m8.shape)

    def phase_b_chunk(t, i, m8):
        r0 = i * kc
        p = jnp.exp2(logits_chunk(t, i) - m8[None])
        p_scr[t % 2][r0:r0 + kc, :] = p.reshape(kc, 2 * tq).astype(BF16)

    def one_pass_chunk(t, i):
        r0 = i * kc
        s = lax.dot_general(k_ref[r0:r0 + kc, :], qs_scr[...], (((1,), (1,)), ((), ())),
                            preferred_element_type=F32)
        b0 = (seq - tq) - t * tq + r0
        bias = bias_scr[b0:b0 + kc, :]
        p = jnp.exp2(s + jnp.concatenate([bias, bias], axis=1))
        p_scr[t % 2][r0:r0 + kc, :] = p.astype(BF16)
        return jnp.sum(p.reshape(kc // sub, sub, 2 * tq), axis=0)

    def finalize(t, chunks, sums=None):
        rows = slice(chunks[0] * kc, (chunks[-1] + 1) * kc)
        if sums is None:
            ot = _dot(vt_scr[:, rows], p_scr[t % 2][rows, :])
            ot = ot[0:V_DIM, :] / ot[V_DIM:V_DIM + 1, :]
        else:
            ot = _dot(vt_scr[0:V_DIM, rows], p_scr[t % 2][rows, :])
            ot = ot / jnp.sum(sums, axis=0, keepdims=True)
        odt = ot[:, 0:tq] - lam * ot[:, tq:2 * tq]
        ms = jnp.mean(odt * odt, axis=0, keepdims=True)
        ont = odt * lax.rsqrt(ms + RMS_EPS) * g_ref[...] * (1.0 - lambda_init)
        o_ref[t * tq:(t + 1) * tq, :] = ont.T.astype(BF16)

    def chunks_of(t, band):
        return [i for i in range(n_kc) if band is None or abs(i * kc // tq - t) <= band]

    def two_pass_pipeline():
        every = chunks_of(0, None)
        for t in range(-1, nq + 1):
            if 0 <= t - 1 < nq:
                finalize(t - 1, every)
            if t + 1 < nq:
                stage_queries(t + 1)
            if 0 <= t < nq:
                m8 = column_max(t, every)
            for i in every:
                if t + 1 < nq:
                    phase_a_chunk(t + 1, i)
                if 0 <= t < nq:
                    phase_b_chunk(t, i, m8)

    def one_pass_pipeline(band):
        sums = None
        for t in range(0, nq + 1):
            if 0 <= t - 1 < nq:
                finalize(t - 1, chunks_of(t - 1, band), sums)
            if t < nq:
                stage_queries(t)
                sums = jnp.zeros((sub, 2 * tq), F32)
                for i in chunks_of(t, band):
                    sums = sums + one_pass_chunk(t, i)

    if not one_pass:
        two_pass_pipeline()
        return

    qf = q_ref[...].astype(F32)
    kf = k_ref[...].astype(F32)
    terms = jnp.concatenate([(qf * qf).astype(BF16), (kf * kf).astype(BF16),
                             (qf * kf).astype(BF16)], axis=1)
    r = lax.broadcasted_iota(jnp.int32, (3 * V_DIM, V_DIM), 0)
    c = lax.broadcasted_iota(jnp.int32, (3 * V_DIM, V_DIM), 1)
    selector = jnp.where(c == r // HEAD_DIM, 1.0, 0.0).astype(BF16)
    sums = _dot(terms, selector)
    top = jnp.max(sums, axis=0, keepdims=True)
    bot = jnp.min(sums, axis=0, keepdims=True)
    lane_s = lax.broadcasted_iota(jnp.int32, (1, V_DIM), 1)
    pick = lambda v, col: jnp.sum(jnp.where(lane_s == col, v, 0.0), axis=1, keepdims=True)
    hi = BOUND_SLACK * jnp.sqrt(jnp.maximum(pick(top, 0) * pick(top, 2),
                                            pick(top, 1) * pick(top, 3)))
    lo = jnp.minimum(pick(bot, 4), pick(bot, 5)) - (BOUND_SLACK - 1.0) * hi
    in_range = jnp.logical_and(hi <= EXP_RANGE, lo >= -EXP_RANGE)
    slope = LOG2E * slopes_ref[head]

    def report(good):
        ok_ref[...] = jnp.broadcast_to(jnp.where(good, 1.0, 0.0), ok_ref.shape)

    if bands is None:
        report(in_range)
        one_pass_pipeline(None)
        return
    for w, band in enumerate(bands):
        @pl.when(pl.program_id(0) == w)
        def _(band=band):
            reach = slope * (band * tq + 1)
            report(jnp.logical_and(in_range, hi - lo + UNDERFLOW <= reach))
            one_pass_pipeline(band)


def _diff_attn(proj, slopes, lq1, lk1, lq2, lk2, subln_g, lambda_init, head0, n_heads,
               one_pass=False, bands=None):
    _, bsz, seq, d = proj.shape
    tq = min(ATTN_TQ, seq)
    vec = lambda n: pl.BlockSpec((1, n), lambda h, b: (0, 0))
    out_specs = pl.BlockSpec((None, seq, V_DIM), lambda h, b: (b, 0, h))
    out_shape = jax.ShapeDtypeStruct((bsz, seq, n_heads * V_DIM), BF16)
    assert bands is None or (one_pass and len(bands) == n_heads)
    if one_pass:
        out_specs = (out_specs,
                     pl.BlockSpec((None, None, SUBLANES, LANES), lambda h, b: (h, b, 0, 0)))
        out_shape = (out_shape, jax.ShapeDtypeStruct((n_heads, bsz, SUBLANES, LANES), F32))
    return pl.pallas_call(
        functools.partial(_diff_attn_kernel, lambda_init, tq, head0, bands, one_pass),
        grid=(n_heads, bsz),
        in_specs=[
            pl.BlockSpec(memory_space=pltpu.SMEM),
            pl.BlockSpec((None, None, seq, V_DIM), lambda h, b: (0, b, 0, head0 + h)),
            pl.BlockSpec((None, None, seq, V_DIM), lambda h, b: (1, b, 0, head0 + h)),
            pl.BlockSpec((None, None, seq, V_DIM), lambda h, b: (2, b, 0, head0 + h)),
            vec(HEAD_DIM), vec(HEAD_DIM), vec(HEAD_DIM), vec(HEAD_DIM),
            pl.BlockSpec((V_DIM, 1), lambda h, b: (0, 0)),
        ],
        out_specs=out_specs,
        out_shape=out_shape,
        scratch_shapes=[
            pltpu.VMEM((2 * seq - tq, tq), F32),
            pltpu.VMEM((V_DIM + BF16_SUBLANES, seq), BF16),
            pltpu.VMEM((2 * tq, V_DIM), BF16),
            pltpu.VMEM((seq, 2 * tq), F32),
            pltpu.VMEM((seq, 2 * tq), F32),
            pltpu.VMEM((seq, 2 * tq), BF16),
            pltpu.VMEM((seq, 2 * tq), BF16),
        ],
        compiler_params=pltpu.CompilerParams(
            dimension_semantics=("arbitrary", "arbitrary"),
            vmem_limit_bytes=VMEM_LIMIT_BYTES),
        name=("diff_attn_two_pass" if not one_pass else
              "diff_attn" if bands is None else "diff_attn_banded"),
    )(slopes, proj, proj, proj, lq1, lk1, lq2, lk2, subln_g)


def _attention(proj, slopes, lq1, lk1, lq2, lk2, subln_g, lambda_init):
    args = (proj, slopes, lq1, lk1, lq2, lk2, subln_g, lambda_init)
    n_band = len(ATTN_BANDS)

    def heads(head0, n_heads, bands):
        fast, ok = _diff_attn(*args, head0, n_heads, one_pass=True, bands=bands)
        return lax.cond(jnp.all(ok > 0.5), lambda: fast,
                        lambda: _diff_attn(*args, head0, n_heads))

    return heads(0, n_band, ATTN_BANDS), heads(n_band, N_HEADS - n_band, None)


def _epilogue_kernel(alpha, on_a_ref, on_b_ref, zs_ref, ga_ref, gb_ref, ya_ref, x_ref, p_ref,
                     wpb_ref, wout_ref, wgate_ref, wple_ref, lng_ref, lnb_ref, out_ref):
    rows = on_a_ref.shape[0] // EPI_SPLIT
    rs = [slice(h * rows, (h + 1) * rows) for h in range(EPI_SPLIT)]
    yb = [_dot(jnp.concatenate([on_a_ref[r, :], on_b_ref[r, :]], axis=1) * zs_ref[r, :],
               wpb_ref[...]) for r in rs]
    pw = [_dot(p_ref[r, :].astype(BF16), wple_ref[...]) for r in rs]
    merged = [(ga_ref[r, :].astype(F32) * ya_ref[r, :].astype(F32)
               + gb_ref[r, :].astype(F32) * y).astype(BF16) for r, y in zip(rs, yb)]
    res = [alpha * x_ref[r, :] + _dot(m, wout_ref[...]) for r, m in zip(rs, merged)]
    gate = [_sigmoid(_dot(v.astype(BF16), wgate_ref[...])) for v in res]
    for r, v, g, w in zip(rs, res, gate, pw):
        v = v + g * w
        mu = jnp.mean(v, axis=-1, keepdims=True)
        rc = v - mu
        var = jnp.mean(rc * rc, axis=-1, keepdims=True)
        out_ref[r, :] = rc * lax.rsqrt(var + LN_EPS) * lng_ref[...] + lnb_ref[...]


def _epilogue(on_a, on_b, proj, ya, x, p, w_proj_b, w_out, w_gate, w_ple, ln_g, ln_b, alpha):
    n, d = x.shape
    ple = p.shape[1]
    t = min(EPI_T, n)
    nt = n // t
    row = lambda w: pl.BlockSpec((t, w), lambda i: (i, 0))
    full = lambda r, c: pl.BlockSpec((r, c), lambda i: (0, 0))
    return pl.pallas_call(
        functools.partial(_epilogue_kernel, alpha),
        grid=(nt,),
        in_specs=[
            row(on_a.shape[1]), row(on_b.shape[1]),
            pl.BlockSpec((None, t, d), lambda i: (3, i, 0)),
            pl.BlockSpec((None, t, d), lambda i: (4, i, 0)),
            pl.BlockSpec((None, t, d), lambda i: (5, i, 0)),
            row(d), row(d), row(ple),
            full(d, d), full(d, d), full(d, d), full(ple, d), full(1, d), full(1, d),
        ],
        out_specs=row(d),
        out_shape=jax.ShapeDtypeStruct((n, d), F32),
        compiler_params=pltpu.CompilerParams(
            dimension_semantics=("arbitrary",), vmem_limit_bytes=VMEM_LIMIT_BYTES),
        name="epilogue",
    )(on_a, on_b, proj, proj, proj, ya, x, p, w_proj_b, w_out, w_gate, w_ple, ln_g, ln_b)


def kernel(x, p, w_in, conv_w, conv_b, w_proj_a, lambda_q1, lambda_k1, lambda_q2, lambda_k2,
           subln_g, w_proj_b, w_out, w_ple, w_ple_gate, ln_g, ln_b):
    depth = w_in.shape[0]
    bsz, seq, d = x.shape
    alpha = (2.0 * depth) ** 0.25
    slopes = jnp.asarray(
        np.float32(2.0) ** (-8.0 * np.arange(1, N_HEADS + 1, dtype=np.float32) / N_HEADS))
    h = x
    for i in range(depth):
        lambda_init = 0.8 - 0.6 * math.exp(-0.3 * i)
        w_in_b = w_in[i].astype(BF16)
        proj = _proj(h.reshape(bsz * seq, d), w_in_b).reshape(N_PROJ + 1, bsz, seq, d)
        ya = _conv_branch(proj, w_in_b, conv_w[i], conv_b[i][None, :], w_proj_a[i].astype(BF16))
        on_a, on_b = _attention(proj, slopes, lambda_q1[i][None, :], lambda_k1[i][None, :],
                                lambda_q2[i][None, :], lambda_k2[i][None, :], subln_g[i][:, None],
                                lambda_init)
        out = _epilogue(on_a.reshape(bsz * seq, -1), on_b.reshape(bsz * seq, -1),
                        proj.reshape(N_PROJ + 1, bsz * seq, d),
                        ya.reshape(bsz * seq, d), h.reshape(bsz * seq, d),
                        p[i].reshape(bsz * seq, -1), w_proj_b[i].astype(BF16),
                        w_out[i].astype(BF16), w_ple_gate[i].astype(BF16),
                        w_ple[i].astype(BF16), ln_g[i][None, :], ln_b[i][None, :], alpha)
        h = out.reshape(bsz, seq, d)
    return h
```

```python
import functools
import math

import jax
import jax.numpy as jnp
import numpy as np
from jax import lax
from jax.experimental import pallas as pl
from jax.experimental.pallas import tpu as pltpu

N_HEADS = 8
HEAD_DIM = 64
V_DIM = 2 * HEAD_DIM
LN_EPS = 1e-5
RMS_EPS = 1e-5
CONV_K = 3
LOG2E = math.log2(math.e)

VMEM_LIMIT_BYTES = 56 * 1024 * 1024
SUBLANES = 8
LANES = 128
BF16_SUBLANES = 16

N_PROJ = 5
PROJ_T = 1024
CONV_CB = 256
ATTN_TQ = 256
ATTN_KC = 256
ATTN_BANDS = (1, 2, 4)
UNDERFLOW = 130.0
BOUND_SLACK = 1.02
EXP_RANGE = 60.0
EPI_T = 1024
EPI_SPLIT = 4

BF16 = jnp.bfloat16
F32 = jnp.float32


def _dot(a, b):
    return jnp.dot(a, b, preferred_element_type=F32)


def _sigmoid(x):
    return 0.5 * jnp.tanh(0.5 * x) + 0.5


def _silu(x):
    return x * _sigmoid(x)


def _proj_kernel(x_ref, wq_ref, wk_ref, wv_ref, wz_ref, wgb_ref, o_ref):
    xb = x_ref[...].astype(BF16)
    o_ref[0] = (_dot(xb, wq_ref[...]) * (HEAD_DIM ** -0.5 * LOG2E)).astype(BF16)
    o_ref[1] = _dot(xb, wk_ref[...]).astype(BF16)
    o_ref[2] = _dot(xb, wv_ref[...]).astype(BF16)
    o_ref[3] = _silu(_dot(xb, wz_ref[...])).astype(BF16)
    o_ref[4] = _sigmoid(_dot(xb, wgb_ref[...])).astype(BF16)
    o_ref[5] = xb


def _proj(x, w_in):
    n, d = x.shape
    t = min(PROJ_T, n)
    wspec = lambda g: pl.BlockSpec((d, d), lambda i, g=g: (0, (4, 5, 6, 7, 9)[g]),
                                   pipeline_mode=pl.Buffered(1))
    return pl.pallas_call(
        _proj_kernel,
        grid=(n // t,),
        in_specs=[pl.BlockSpec((t, d), lambda i: (i, 0))] + [wspec(g) for g in range(N_PROJ)],
        out_specs=pl.BlockSpec((N_PROJ + 1, t, d), lambda i: (0, i, 0)),
        out_shape=jax.ShapeDtypeStruct((N_PROJ + 1, n, d), BF16),
        compiler_params=pltpu.CompilerParams(
            dimension_semantics=("arbitrary",), vmem_limit_bytes=VMEM_LIMIT_BYTES),
        name="proj",
    )(x, *([w_in] * N_PROJ))


def _conv_branch_kernel(x_ref, wu_ref, wc_ref, wb_ref, wz_ref, wga_ref, cw_ref, cb_ref, wpa_ref,
                        ya_ref, y_scr):
    seq = x_ref.shape[0]
    width = wpa_ref.shape[0]
    xb = x_ref[...]
    row = lax.broadcasted_iota(jnp.int32, (seq, CONV_CB), 0)
    for c0 in range(0, width, CONV_CB):
        cols = slice(c0, c0 + CONV_CB)
        h = _dot(xb, wc_ref[:, cols]) * _dot(xb, wu_ref[:, cols])
        h_prev = jnp.where(row == 0, 0.0, pltpu.roll(h, 1, axis=0))
        h_next = jnp.where(row == seq - 1, 0.0, pltpu.roll(h, seq - 1, axis=0))
        conv = (cw_ref[0:1, cols] * h_prev + cw_ref[1:2, cols] * h + cw_ref[2:3, cols] * h_next
                + cb_ref[:, cols])
        y = _dot(xb, wb_ref[:, cols]) * conv * _silu(_dot(xb, wz_ref[:, cols]))
        y_scr[:, cols] = y.astype(BF16)
    gate = _sigmoid(_dot(xb, wga_ref[...]))
    ya_ref[...] = (gate * _dot(y_scr[...], wpa_ref[...])).astype(BF16)


def _conv_branch(proj, w_in, conv_w, conv_b, w_proj_a):
    _, bsz, seq, d = proj.shape
    width = w_proj_a.shape[0]
    wspec = lambda g: pl.BlockSpec((d, width), lambda b, g=g: (0, g),
                                   pipeline_mode=pl.Buffered(1))
    const = lambda shape: pl.BlockSpec(shape, lambda b: (0, 0), pipeline_mode=pl.Buffered(1))
    return pl.pallas_call(
        _conv_branch_kernel,
        grid=(bsz,),
        in_specs=[
            pl.BlockSpec((None, None, seq, d), lambda b: (N_PROJ, b, 0, 0)),
            wspec(0), wspec(1), wspec(2), wspec(3),
            pl.BlockSpec((d, d), lambda b: (0, 8), pipeline_mode=pl.Buffered(1)),
            const((CONV_K, width)), const((1, width)), const((width, d)),
        ],
        out_specs=pl.BlockSpec((None, seq, d), lambda b: (b, 0, 0)),
        out_shape=jax.ShapeDtypeStruct((bsz, seq, d), BF16),
        scratch_shapes=[pltpu.VMEM((seq, width), BF16)],
        compiler_params=pltpu.CompilerParams(
            dimension_semantics=("arbitrary",), vmem_limit_bytes=VMEM_LIMIT_BYTES),
        name="conv_branch",
    )(proj, w_in, w_in, w_in, w_in, w_in, conv_w, conv_b, w_proj_a)


def _diff_attn_kernel(lambda_init, tq, head0, bands, one_pass, slopes_ref, q_ref, k_ref, v_ref,
                      lq1_ref, lk1_ref, lq2_ref, lk2_ref, g_ref, o_ref, *rest):
    if one_pass:
        ok_ref, bias_scr, vt_scr, qs_scr, s0_scr, s1_scr, p0_scr, p1_scr = rest
    else:
        ok_ref = None
        bias_scr, vt_scr, qs_scr, s0_scr, s1_scr, p0_scr, p1_scr = rest
    s_scr = (s0_scr, s1_scr)
    p_scr = (p0_scr, p1_scr)
    head = head0 + pl.program_id(0)
    b = pl.program_id(1)
    seq = k_ref.shape[0]
    nq = seq // tq
    kc = min(ATTN_KC, seq)
    n_kc = seq // kc
    sub = SUBLANES

    @pl.when(b == 0)
    def _():
        r = lax.broadcasted_iota(jnp.int32, bias_scr.shape, 0)
        i = lax.broadcasted_iota(jnp.int32, bias_scr.shape, 1)
        dist = jnp.abs(r - (seq - tq) - i).astype(F32)
        bias_scr[...] = dist * (-LOG2E * slopes_ref[head])

        vt_scr[V_DIM:V_DIM + BF16_SUBLANES, :] = jnp.where(
            lax.broadcasted_iota(jnp.int32, (BF16_SUBLANES, seq), 0) == 0, 1.0, 0.0).astype(BF16)

    vt_scr[0:V_DIM, :] = v_ref[...].T

    lam = (jnp.exp(jnp.sum(lq1_ref[...] * lk1_ref[...], keepdims=True))
           - jnp.exp(jnp.sum(lq2_ref[...] * lk2_ref[...], keepdims=True))
           + lambda_init)

    lane = lax.broadcasted_iota(jnp.int32, (1, V_DIM), 1)
    map1_lanes = jnp.where(lane < HEAD_DIM, 1.0, 0.0).astype(BF16)
    map2_lanes = jnp.where(lane >= HEAD_DIM, 1.0, 0.0).astype(BF16)

    def stage_queries(t):
        q = q_ref[t * tq:(t + 1) * tq, :]
        qs_scr[...] = jnp.concatenate([q * map1_lanes, q * map2_lanes], axis=0)

    def phase_a_chunk(t, i):
        r0 = i * kc
        s = lax.dot_general(k_ref[r0:r0 + kc, :], qs_scr[...], (((1,), (1,)), ((), ())),
                            preferred_element_type=F32)
        b0 = (seq - tq) - t * tq + r0
        bias = bias_scr[b0:b0 + kc, :]
        s_scr[t % 2][r0:r0 + kc, :] = s + jnp.concatenate([bias, bias], axis=1)

    def logits_chunk(t, i):
        return s_scr[t % 2][i * kc:(i + 1) * kc, :].reshape(kc // sub, sub, 2 * tq)

    def column_max(t, chunks):
        m8 = jnp.full((sub, 2 * tq), -jnp.inf, F32)
        for i in chunks:
            m8 = jnp.maximum(m8, jnp.max(logits_chunk(t, i), axis=0))
        return jnp.broadcast_to(jnp.max(m8, axis=0, keepdims=True), m8.shape)

    def phase_b_chunk(t, i, m8):
        r0 = i * kc
        p = jnp.exp2(logits_chunk(t, i) - m8[None])
        p_scr[t % 2][r0:r0 + kc, :] = p.reshape(kc, 2 * tq).astype(BF16)

    def one_pass_chunk(t, i):
        r0 = i * kc
        s = lax.dot_general(k_ref[r0:r0 + kc, :], qs_scr[...], (((1,), (1,)), ((), ())),
                            preferred_element_type=F32)
        b0 = (seq - tq) - t * tq + r0
        bias = bias_scr[b0:b0 + kc, :]
        p = jnp.exp2(s + jnp.concatenate([bias, bias], axis=1))
        p_scr[t % 2][r0:r0 + kc, :] = p.astype(BF16)
        return p

    def finalize(t, chunks, sums=None):
        rows = slice(chunks[0] * kc, (chunks[-1] + 1) * kc)
        if sums is None:
            ot = _dot(vt_scr[:, rows], p_scr[t % 2][rows, :])
            ot = ot[0:V_DIM, :] / ot[V_DIM:V_DIM + 1, :]
        else:
            ot = _dot(vt_scr[0:V_DIM, rows], p_scr[t % 2][rows, :])
            ot = ot / jnp.sum(sums, axis=0, keepdims=True)
        odt = ot[:, 0:tq] - lam * ot[:, tq:2 * tq]
        ms = jnp.mean(odt * odt, axis=0, keepdims=True)
        ont = odt * lax.rsqrt(ms + RMS_EPS) * g_ref[...] * (1.0 - lambda_init)
        o_ref[t * tq:(t + 1) * tq, :] = ont.T.astype(BF16)

    def chunks_of(t, band):
        return [i for i in range(n_kc) if band is None or abs(i * kc // tq - t) <= band]

    def two_pass_pipeline():
        every = chunks_of(0, None)
        for t in range(-1, nq + 1):
            if 0 <= t - 1 < nq:
                finalize(t - 1, every)
            if t + 1 < nq:
                stage_queries(t + 1)
            if 0 <= t < nq:
                m8 = column_max(t, every)
            for i in every:
                if t + 1 < nq:
                    phase_a_chunk(t + 1, i)
                if 0 <= t < nq:
                    phase_b_chunk(t, i, m8)

    def one_pass_pipeline(band):
        sums = None
        for t in range(0, nq + 1):
            if 0 <= t - 1 < nq:
                finalize(t - 1, chunks_of(t - 1, band), sums)
            if t < nq:
                stage_queries(t)
                if band is not None:
                    sums = jnp.zeros((sub, 2 * tq), F32)
                for i in chunks_of(t, band):
                    p = one_pass_chunk(t, i)
                    if band is not None:
                        sums = sums + jnp.sum(p.reshape(kc // sub, sub, 2 * tq), axis=0)

    if not one_pass:
        two_pass_pipeline()
        return

    qf = q_ref[...].astype(F32)
    kf = k_ref[...].astype(F32)
    terms = jnp.concatenate([(qf * qf).astype(BF16), (kf * kf).astype(BF16),
                             (qf * kf).astype(BF16)], axis=1)
    r = lax.broadcasted_iota(jnp.int32, (3 * V_DIM, V_DIM), 0)
    c = lax.broadcasted_iota(jnp.int32, (3 * V_DIM, V_DIM), 1)
    selector = jnp.where(c == r // HEAD_DIM, 1.0, 0.0).astype(BF16)
    sums = _dot(terms, selector)
    top = jnp.max(sums, axis=0, keepdims=True)
    bot = jnp.min(sums, axis=0, keepdims=True)
    lane_s = lax.broadcasted_iota(jnp.int32, (1, V_DIM), 1)
    pick = lambda v, col: jnp.sum(jnp.where(lane_s == col, v, 0.0), axis=1, keepdims=True)
    hi = BOUND_SLACK * jnp.sqrt(jnp.maximum(pick(top, 0) * pick(top, 2),
                                            pick(top, 1) * pick(top, 3)))
    lo = jnp.minimum(pick(bot, 4), pick(bot, 5)) - (BOUND_SLACK - 1.0) * hi
    in_range = jnp.logical_and(hi <= EXP_RANGE, lo >= -EXP_RANGE)
    slope = LOG2E * slopes_ref[head]

    def report(good):
        ok_ref[...] = jnp.broadcast_to(jnp.where(good, 1.0, 0.0), ok_ref.shape)

    if bands is None:
        report(in_range)
        one_pass_pipeline(None)
        return
    for w, band in enumerate(bands):
        @pl.when(pl.program_id(0) == w)
        def _(band=band):
            reach = slope * (band * tq + 1)
            report(jnp.logical_and(in_range, hi - lo + UNDERFLOW <= reach))
            one_pass_pipeline(band)


def _diff_attn(proj, slopes, lq1, lk1, lq2, lk2, subln_g, lambda_init, head0, n_heads,
               one_pass=False, bands=None):
    _, bsz, seq, d = proj.shape
    tq = min(ATTN_TQ, seq)
    vec = lambda n: pl.BlockSpec((1, n), lambda h, b: (0, 0))
    out_specs = pl.BlockSpec((None, seq, V_DIM), lambda h, b: (b, 0, h))
    out_shape = jax.ShapeDtypeStruct((bsz, seq, n_heads * V_DIM), BF16)
    assert bands is None or (one_pass and len(bands) == n_heads)
    if one_pass:
        out_specs = (out_specs,
                     pl.BlockSpec((None, None, SUBLANES, LANES), lambda h, b: (h, b, 0, 0)))
        out_shape = (out_shape, jax.ShapeDtypeStruct((n_heads, bsz, SUBLANES, LANES), F32))
    return pl.pallas_call(
        functools.partial(_diff_attn_kernel, lambda_init, tq, head0, bands, one_pass),
        grid=(n_heads, bsz),
        in_specs=[
            pl.BlockSpec(memory_space=pltpu.SMEM),
            pl.BlockSpec((None, None, seq, V_DIM), lambda h, b: (0, b, 0, head0 + h)),
            pl.BlockSpec((None, None, seq, V_DIM), lambda h, b: (1, b, 0, head0 + h)),
            pl.BlockSpec((None, None, seq, V_DIM), lambda h, b: (2, b, 0, head0 + h)),
            vec(HEAD_DIM), vec(HEAD_DIM), vec(HEAD_DIM), vec(HEAD_DIM),
            pl.BlockSpec((V_DIM, 1), lambda h, b: (0, 0)),
        ],
        out_specs=out_specs,
        out_shape=out_shape,
        scratch_shapes=[
            pltpu.VMEM((2 * seq - tq, tq), F32),
            pltpu.VMEM((V_DIM + BF16_SUBLANES, seq), BF16),
            pltpu.VMEM((2 * tq, V_DIM), BF16),
            pltpu.VMEM((seq, 2 * tq), F32),
            pltpu.VMEM((seq, 2 * tq), F32),
            pltpu.VMEM((seq, 2 * tq), BF16),
            pltpu.VMEM((seq, 2 * tq), BF16),
        ],
        compiler_params=pltpu.CompilerParams(
            dimension_semantics=("arbitrary", "arbitrary"),
            vmem_limit_bytes=VMEM_LIMIT_BYTES),
        name=("diff_attn_two_pass" if not one_pass else
              "diff_attn" if bands is None else "diff_attn_banded"),
    )(slopes, proj, proj, proj, lq1, lk1, lq2, lk2, subln_g)


def _attention(proj, slopes, lq1, lk1, lq2, lk2, subln_g, lambda_init):
    args = (proj, slopes, lq1, lk1, lq2, lk2, subln_g, lambda_init)
    n_band = len(ATTN_BANDS)

    def heads(head0, n_heads, bands):
        fast, ok = _diff_attn(*args, head0, n_heads, one_pass=True, bands=bands)
        return lax.cond(jnp.all(ok > 0.5), lambda: fast,
                        lambda: _diff_attn(*args, head0, n_heads))

    return heads(0, n_band, ATTN_BANDS), heads(n_band, N_HEADS - n_band, None)


def _epilogue_kernel(alpha, on_a_ref, on_b_ref, zs_ref, gb_ref, ya_ref, x_ref, p_ref,
                     wpb_ref, wout_ref, wgate_ref, wple_ref, lng_ref, lnb_ref, out_ref):
    rows = on_a_ref.shape[0] // EPI_SPLIT
    rs = [slice(h * rows, (h + 1) * rows) for h in range(EPI_SPLIT)]
    yb = [_dot(jnp.concatenate([on_a_ref[r, :], on_b_ref[r, :]], axis=1) * zs_ref[r, :],
               wpb_ref[...]) for r in rs]
    pw = [_dot(p_ref[r, :].astype(BF16), wple_ref[...]) for r in rs]
    merged = [(ya_ref[r, :].astype(F32) + gb_ref[r, :].astype(F32) * y).astype(BF16)
              for r, y in zip(rs, yb)]
    res = [alpha * x_ref[r, :] + _dot(m, wout_ref[...]) for r, m in zip(rs, merged)]
    gate = [_sigmoid(_dot(v.astype(BF16), wgate_ref[...])) for v in res]
    for r, v, g, w in zip(rs, res, gate, pw):
        v = v + g * w
        mu = jnp.mean(v, axis=-1, keepdims=True)
        rc = v - mu
        var = jnp.mean(rc * rc, axis=-1, keepdims=True)
        out_ref[r, :] = rc * lax.rsqrt(var + LN_EPS) * lng_ref[...] + lnb_ref[...]


def _epilogue(on_a, on_b, proj, ya, x, p, w_proj_b, w_out, w_gate, w_ple, ln_g, ln_b, alpha):
    n, d = x.shape
    ple = p.shape[1]
    t = min(EPI_T, n)
    nt = n // t
    row = lambda w: pl.BlockSpec((t, w), lambda i: (i, 0))
    full = lambda r, c: pl.BlockSpec((r, c), lambda i: (0, 0))
    return pl.pallas_call(
        functools.partial(_epilogue_kernel, alpha),
        grid=(nt,),
        in_specs=[
            row(on_a.shape[1]), row(on_b.shape[1]),
            pl.BlockSpec((None, t, d), lambda i: (3, i, 0)),
            pl.BlockSpec((None, t, d), lambda i: (4, i, 0)),
            row(d), row(d), row(ple),
            full(d, d), full(d, d), full(d, d), full(ple, d), full(1, d), full(1, d),
        ],
        out_specs=row(d),
        out_shape=jax.ShapeDtypeStruct((n, d), F32),
        compiler_params=pltpu.CompilerParams(
            dimension_semantics=("arbitrary",), vmem_limit_bytes=VMEM_LIMIT_BYTES),
        name="epilogue",
    )(on_a, on_b, proj, proj, ya, x, p, w_proj_b, w_out, w_gate, w_ple, ln_g, ln_b)


def kernel(x, p, w_in, conv_w, conv_b, w_proj_a, lambda_q1, lambda_k1, lambda_q2, lambda_k2,
           subln_g, w_proj_b, w_out, w_ple, w_ple_gate, ln_g, ln_b):
    depth = w_in.shape[0]
    bsz, seq, d = x.shape
    alpha = (2.0 * depth) ** 0.25
    slopes = jnp.asarray(
        np.float32(2.0) ** (-8.0 * np.arange(1, N_HEADS + 1, dtype=np.float32) / N_HEADS))
    h = x
    for i in range(depth):
        lambda_init = 0.8 - 0.6 * math.exp(-0.3 * i)
        w_in_b = w_in[i].astype(BF16)
        proj = _proj(h.reshape(bsz * seq, d), w_in_b).reshape(N_PROJ + 1, bsz, seq, d)
        ya = _conv_branch(proj, w_in_b, conv_w[i], conv_b[i][None, :], w_proj_a[i].astype(BF16))
        on_a, on_b = _attention(proj, slopes, lambda_q1[i][None, :], lambda_k1[i][None, :],
                                lambda_q2[i][None, :], lambda_k2[i][None, :], subln_g[i][:, None],
                                lambda_init)
        out = _epilogue(on_a.reshape(bsz * seq, -1), on_b.reshape(bsz * seq, -1),
                        proj.reshape(N_PROJ + 1, bsz * seq, d),
                        ya.reshape(bsz * seq, d), h.reshape(bsz * seq, d),
                        p[i].reshape(bsz * seq, -1), w_proj_b[i].astype(BF16),
                        w_out[i].astype(BF16), w_ple_gate[i].astype(BF16),
                        w_ple[i].astype(BF16), ln_g[i][None, :], ln_b[i][None, :], alpha)
        h = out.reshape(bsz, seq, d)
    return h
```

```python
import functools
import math

import jax
import jax.numpy as jnp
import numpy as np
from jax import lax
from jax.experimental import pallas as pl
from jax.experimental.pallas import tpu as pltpu

N_HEADS = 8
HEAD_DIM = 64
V_DIM = 2 * HEAD_DIM
LN_EPS = 1e-5
RMS_EPS = 1e-5
CONV_K = 3
LOG2E = math.log2(math.e)

VMEM_LIMIT_BYTES = 56 * 1024 * 1024
SUBLANES = 8
LANES = 128
BF16_SUBLANES = 16

N_PROJ = 6
PROJ_T = 1024
CONV_CB = 256
ATTN_TQ = 256
ATTN_KC = 256
ATTN_BANDS = (1, 2, 4)
UNDERFLOW = 130.0
BOUND_SLACK = 1.02
EXP_RANGE = 60.0
EPI_T = 1024
EPI_SPLIT = 4

BF16 = jnp.bfloat16
F32 = jnp.float32


def _dot(a, b):
    return jnp.dot(a, b, preferred_element_type=F32)


def _sigmoid(x):
    return 0.5 * jnp.tanh(0.5 * x) + 0.5


def _silu(x):
    return x * _sigmoid(x)


def _proj_kernel(x_ref, wq_ref, wk_ref, wv_ref, wz_ref, wga_ref, wgb_ref, o_ref):
    xb = x_ref[...].astype(BF16)
    o_ref[0] = (_dot(xb, wq_ref[...]) * (HEAD_DIM ** -0.5 * LOG2E)).astype(BF16)
    o_ref[1] = _dot(xb, wk_ref[...]).astype(BF16)
    o_ref[2] = _dot(xb, wv_ref[...]).astype(BF16)
    o_ref[3] = _silu(_dot(xb, wz_ref[...])).astype(BF16)
    o_ref[4] = _sigmoid(_dot(xb, wga_ref[...])).astype(BF16)
    o_ref[5] = _sigmoid(_dot(xb, wgb_ref[...])).astype(BF16)
    o_ref[6] = xb


def _proj(x, w_in):
    n, d = x.shape
    t = min(PROJ_T, n)
    wspec = lambda g: pl.BlockSpec((d, d), lambda i, g=g: (0, 4 + g), pipeline_mode=pl.Buffered(1))
    return pl.pallas_call(
        _proj_kernel,
        grid=(n // t,),
        in_specs=[pl.BlockSpec((t, d), lambda i: (i, 0))] + [wspec(g) for g in range(N_PROJ)],
        out_specs=pl.BlockSpec((N_PROJ + 1, t, d), lambda i: (0, i, 0)),
        out_shape=jax.ShapeDtypeStruct((N_PROJ + 1, n, d), BF16),
        compiler_params=pltpu.CompilerParams(
            dimension_semantics=("arbitrary",), vmem_limit_bytes=VMEM_LIMIT_BYTES),
        name="proj",
    )(x, *([w_in] * N_PROJ))


def _conv_branch_kernel(x_ref, wu_ref, wc_ref, wb_ref, wz_ref, cw_ref, cb_ref, wpa_ref, ya_ref,
                        y_scr):
    seq = x_ref.shape[0]
    width = wpa_ref.shape[0]
    xb = x_ref[...]
    row = lax.broadcasted_iota(jnp.int32, (seq, CONV_CB), 0)
    for c0 in range(0, width, CONV_CB):
        cols = slice(c0, c0 + CONV_CB)
        h = _dot(xb, wc_ref[:, cols]) * _dot(xb, wu_ref[:, cols])
        h_prev = jnp.where(row == 0, 0.0, pltpu.roll(h, 1, axis=0))
        h_next = jnp.where(row == seq - 1, 0.0, pltpu.roll(h, seq - 1, axis=0))
        conv = (cw_ref[0:1, cols] * h_prev + cw_ref[1:2, cols] * h + cw_ref[2:3, cols] * h_next
                + cb_ref[:, cols])
        y = _dot(xb, wb_ref[:, cols]) * conv * _silu(_dot(xb, wz_ref[:, cols]))
        y_scr[:, cols] = y.astype(BF16)
    ya_ref[...] = _dot(y_scr[...], wpa_ref[...]).astype(BF16)


def _conv_branch(proj, w_in, conv_w, conv_b, w_proj_a):
    _, bsz, seq, d = proj.shape
    width = w_proj_a.shape[0]
    wspec = lambda g: pl.BlockSpec((d, width), lambda b, g=g: (0, g),
                                   pipeline_mode=pl.Buffered(1))
    const = lambda shape: pl.BlockSpec(shape, lambda b: (0, 0), pipeline_mode=pl.Buffered(1))
    return pl.pallas_call(
        _conv_branch_kernel,
        grid=(bsz,),
        in_specs=[
            pl.BlockSpec((None, None, seq, d), lambda b: (N_PROJ, b, 0, 0)),
            wspec(0), wspec(1), wspec(2), wspec(3),
            const((CONV_K, width)), const((1, width)), const((width, d)),
        ],
        out_specs=pl.BlockSpec((None, seq, d), lambda b: (b, 0, 0)),
        out_shape=jax.ShapeDtypeStruct((bsz, seq, d), BF16),
        scratch_shapes=[pltpu.VMEM((seq, width), BF16)],
        compiler_params=pltpu.CompilerParams(
            dimension_semantics=("arbitrary",), vmem_limit_bytes=VMEM_LIMIT_BYTES),
        name="conv_branch",
    )(proj, w_in, w_in, w_in, w_in, conv_w, conv_b, w_proj_a)


def _diff_attn_kernel(lambda_init, tq, head0, bands, one_pass, slopes_ref, q_ref, k_ref, v_ref,
                      lq1_ref, lk1_ref, lq2_ref, lk2_ref, g_ref, o_ref, *rest):
    if one_pass:
        ok_ref, bias_scr, vt_scr, qs_scr, s0_scr, s1_scr, p0_scr, p1_scr = rest
    else:
        ok_ref = None
        bias_scr, vt_scr, qs_scr, s0_scr, s1_scr, p0_scr, p1_scr = rest
    s_scr = (s0_scr, s1_scr)
    p_scr = (p0_scr, p1_scr)
    head = head0 + pl.program_id(0)
    b = pl.program_id(1)
    seq = k_ref.shape[0]
    nq = seq // tq
    kc = min(ATTN_KC, seq)
    n_kc = seq // kc
    sub = SUBLANES

    @pl.when(b == 0)
    def _():
        r = lax.broadcasted_iota(jnp.int32, bias_scr.shape, 0)
        i = lax.broadcasted_iota(jnp.int32, bias_scr.shape, 1)
        dist = jnp.abs(r - (seq - tq) - i).astype(F32)
        bias_scr[...] = dist * (-LOG2E * slopes_ref[head])

        vt_scr[V_DIM:V_DIM + BF16_SUBLANES, :] = jnp.where(
            lax.broadcasted_iota(jnp.int32, (BF16_SUBLANES, seq), 0) == 0, 1.0, 0.0).astype(BF16)

    vt_scr[0:V_DIM, :] = v_ref[...].T

    lam = (jnp.exp(jnp.sum(lq1_ref[...] * lk1_ref[...], keepdims=True))
           - jnp.exp(jnp.sum(lq2_ref[...] * lk2_ref[...], keepdims=True))
           + lambda_init)

    lane = lax.broadcasted_iota(jnp.int32, (1, V_DIM), 1)
    map1_lanes = jnp.where(lane < HEAD_DIM, 1.0, 0.0).astype(BF16)
    map2_lanes = jnp.where(lane >= HEAD_DIM, 1.0, 0.0).astype(BF16)

    def stage_queries(t):
        q = q_ref[t * tq:(t + 1) * tq, :]
        qs_scr[...] = jnp.concatenate([q * map1_lanes, q * map2_lanes], axis=0)

    def phase_a_chunk(t, i):
        r0 = i * kc
        s = lax.dot_general(k_ref[r0:r0 + kc, :], qs_scr[...], (((1,), (1,)), ((), ())),
                            preferred_element_type=F32)
        b0 = (seq - tq) - t * tq + r0
        bias = bias_scr[b0:b0 + kc, :]
        s_scr[t % 2][r0:r0 + kc, :] = s + jnp.concatenate([bias, bias], axis=1)

    def logits_chunk(t, i):
        return s_scr[t % 2][i * kc:(i + 1) * kc, :].reshape(kc // sub, sub, 2 * tq)

    def column_max(t, chunks):
        m8 = jnp.full((sub, 2 * tq), -jnp.inf, F32)
        for i in chunks:
            m8 = jnp.maximum(m8, jnp.max(logits_chunk(t, i), axis=0))
        return jnp.broadcast_to(jnp.max(m8, axis=0, keepdims=True), m8.shape)

    def phase_b_chunk(t, i, m8):
        r0 = i * kc
        p = jnp.exp2(logits_chunk(t, i) - m8[None])
        p_scr[t % 2][r0:r0 + kc, :] = p.reshape(kc, 2 * tq).astype(BF16)

    def one_pass_chunk(t, i):
        r0 = i * kc
        s = lax.dot_general(k_ref[r0:r0 + kc, :], qs_scr[...], (((1,), (1,)), ((), ())),
                            preferred_element_type=F32)
        b0 = (seq - tq) - t * tq + r0
        bias = bias_scr[b0:b0 + kc, :]
        p = jnp.exp2(s + jnp.concatenate([bias, bias], axis=1))
        p_scr[t % 2][r0:r0 + kc, :] = p.astype(BF16)
        return p

    def finalize(t, chunks, sums=None):
        rows = slice(chunks[0] * kc, (chunks[-1] + 1) * kc)
        if sums is None:
            ot = _dot(vt_scr[:, rows], p_scr[t % 2][rows, :])
            ot = ot[0:V_DIM, :] / ot[V_DIM:V_DIM + 1, :]
        else:
            ot = _dot(vt_scr[0:V_DIM, rows], p_scr[t % 2][rows, :])
            ot = ot / jnp.sum(sums, axis=0, keepdims=True)
        odt = ot[:, 0:tq] - lam * ot[:, tq:2 * tq]
        ms = jnp.mean(odt * odt, axis=0, keepdims=True)
        ont = odt * lax.rsqrt(ms + RMS_EPS) * g_ref[...] * (1.0 - lambda_init)
        o_ref[t * tq:(t + 1) * tq, :] = ont.T.astype(BF16)

    def chunks_of(t, band):
        return [i for i in range(n_kc) if band is None or abs(i * kc // tq - t) <= band]

    def two_pass_pipeline():
        every = chunks_of(0, None)
        for t in range(-1, nq + 1):
            if 0 <= t - 1 < nq:
                finalize(t - 1, every)
            if t + 1 < nq:
                stage_queries(t + 1)
            if 0 <= t < nq:
                m8 = column_max(t, every)
            for i in every:
                if t + 1 < nq:
                    phase_a_chunk(t + 1, i)
                if 0 <= t < nq:
                    phase_b_chunk(t, i, m8)

    def one_pass_pipeline(band):
        sums = None
        for t in range(0, nq + 1):
            if 0 <= t - 1 < nq:
                finalize(t - 1, chunks_of(t - 1, band), sums)
            if t < nq:
                stage_queries(t)
                if band is not None:
                    sums = jnp.zeros((sub, 2 * tq), F32)
                for i in chunks_of(t, band):
                    p = one_pass_chunk(t, i)
                    if band is not None:
                        sums = sums + jnp.sum(p.reshape(kc // sub, sub, 2 * tq), axis=0)

    if not one_pass:
        two_pass_pipeline()
        return

    qf = q_ref[...].astype(F32)
    kf = k_ref[...].astype(F32)
    terms = jnp.concatenate([(qf * qf).astype(BF16), (kf * kf).astype(BF16),
                             (qf * kf).astype(BF16)], axis=1)
    r = lax.broadcasted_iota(jnp.int32, (3 * V_DIM, V_DIM), 0)
    c = lax.broadcasted_iota(jnp.int32, (3 * V_DIM, V_DIM), 1)
    selector = jnp.where(c == r // HEAD_DIM, 1.0, 0.0).astype(BF16)
    sums = _dot(terms, selector)
    top = jnp.max(sums, axis=0, keepdims=True)
    bot = jnp.min(sums, axis=0, keepdims=True)
    lane_s = lax.broadcasted_iota(jnp.int32, (1, V_DIM), 1)
    pick = lambda v, col: jnp.sum(jnp.where(lane_s == col, v, 0.0), axis=1, keepdims=True)
    hi = BOUND_SLACK * jnp.sqrt(jnp.maximum(pick(top, 0) * pick(top, 2),
                                            pick(top, 1) * pick(top, 3)))
    lo = jnp.minimum(pick(bot, 4), pick(bot, 5)) - (BOUND_SLACK - 1.0) * hi
    in_range = jnp.logical_and(hi <= EXP_RANGE, lo >= -EXP_RANGE)
    slope = LOG2E * slopes_ref[head]

    def report(good):
        ok_ref[...] = jnp.broadcast_to(jnp.where(good, 1.0, 0.0), ok_ref.shape)

    if bands is None:
        report(in_range)
        one_pass_pipeline(None)
        return
    for w, band in enumerate(bands):
        @pl.when(pl.program_id(0) == w)
        def _(band=band):
            reach = slope * (band * tq + 1)
            report(jnp.logical_and(in_range, hi - lo + UNDERFLOW <= reach))
            one_pass_pipeline(band)


def _diff_attn(proj, slopes, lq1, lk1, lq2, lk2, subln_g, lambda_init, head0, n_heads,
               one_pass=False, bands=None):
    _, bsz, seq, d = proj.shape
    tq = min(ATTN_TQ, seq)
    vec = lambda n: pl.BlockSpec((1, n), lambda h, b: (0, 0))
    out_specs = pl.BlockSpec((None, seq, V_DIM), lambda h, b: (b, 0, h))
    out_shape = jax.ShapeDtypeStruct((bsz, seq, n_heads * V_DIM), BF16)
    assert bands is None or (one_pass and len(bands) == n_heads)
    if one_pass:
        out_specs = (out_specs,
                     pl.BlockSpec((None, None, SUBLANES, LANES), lambda h, b: (h, b, 0, 0)))
        out_shape = (out_shape, jax.ShapeDtypeStruct((n_heads, bsz, SUBLANES, LANES), F32))
    return pl.pallas_call(
        functools.partial(_diff_attn_kernel, lambda_init, tq, head0, bands, one_pass),
        grid=(n_heads, bsz),
        in_specs=[
            pl.BlockSpec(memory_space=pltpu.SMEM),
            pl.BlockSpec((None, None, seq, V_DIM), lambda h, b: (0, b, 0, head0 + h)),
            pl.BlockSpec((None, None, seq, V_DIM), lambda h, b: (1, b, 0, head0 + h)),
            pl.BlockSpec((None, None, seq, V_DIM), lambda h, b: (2, b, 0, head0 + h)),
            vec(HEAD_DIM), vec(HEAD_DIM), vec(HEAD_DIM), vec(HEAD_DIM),
            pl.BlockSpec((V_DIM, 1), lambda h, b: (0, 0)),
        ],
        out_specs=out_specs,
        out_shape=out_shape,
        scratch_shapes=[
            pltpu.VMEM((2 * seq - tq, tq), F32),
            pltpu.VMEM((V_DIM + BF16_SUBLANES, seq), BF16),
            pltpu.VMEM((2 * tq, V_DIM), BF16),
            pltpu.VMEM((seq, 2 * tq), F32),
            pltpu.VMEM((seq, 2 * tq), F32),
            pltpu.VMEM((seq, 2 * tq), BF16),
            pltpu.VMEM((seq, 2 * tq), BF16),
        ],
        compiler_params=pltpu.CompilerParams(
            dimension_semantics=("arbitrary", "arbitrary"),
            vmem_limit_bytes=VMEM_LIMIT_BYTES),
        name=("diff_attn_two_pass" if not one_pass else
              "diff_attn" if bands is None else "diff_attn_banded"),
    )(slopes, proj, proj, proj, lq1, lk1, lq2, lk2, subln_g)


def _attention(proj, slopes, lq1, lk1, lq2, lk2, subln_g, lambda_init):
    args = (proj, slopes, lq1, lk1, lq2, lk2, subln_g, lambda_init)
    n_band = len(ATTN_BANDS)

    def heads(head0, n_heads, bands):
        fast, ok = _diff_attn(*args, head0, n_heads, one_pass=True, bands=bands)
        return lax.cond(jnp.all(ok > 0.5), lambda: fast,
                        lambda: _diff_attn(*args, head0, n_heads))

    return heads(0, n_band, ATTN_BANDS), heads(n_band, N_HEADS - n_band, None)


def _epilogue_kernel(alpha, on_a_ref, on_b_ref, zs_ref, ga_ref, gb_ref, ya_ref, x_ref, p_ref,
                     wpb_ref, wout_ref, wgate_ref, wple_ref, lng_ref, lnb_ref, out_ref):
    rows = on_a_ref.shape[0] // EPI_SPLIT
    rs = [slice(h * rows, (h + 1) * rows) for h in range(EPI_SPLIT)]
    yb = [_dot(jnp.concatenate([on_a_ref[r, :], on_b_ref[r, :]], axis=1) * zs_ref[r, :],
               wpb_ref[...]) for r in rs]
    pw = [_dot(p_ref[r, :].astype(BF16), wple_ref[...]) for r in rs]
    merged = [(ga_ref[r, :].astype(F32) * ya_ref[r, :].astype(F32)
               + gb_ref[r, :].astype(F32) * y).astype(BF16) for r, y in zip(rs, yb)]
    res = [alpha * x_ref[r, :] + _dot(m, wout_ref[...]) for r, m in zip(rs, merged)]
    gate = [_sigmoid(_dot(v.astype(BF16), wgate_ref[...])) for v in res]
    for r, v, g, w in zip(rs, res, gate, pw):
        v = v + g * w
        mu = jnp.mean(v, axis=-1, keepdims=True)
        rc = v - mu
        var = jnp.mean(rc * rc, axis=-1, keepdims=True)
        out_ref[r, :] = rc * lax.rsqrt(var + LN_EPS) * lng_ref[...] + lnb_ref[...]


def _epilogue(on_a, on_b, proj, ya, x, p, w_proj_b, w_out, w_gate, w_ple, ln_g, ln_b, alpha):
    n, d = x.shape
    ple = p.shape[1]
    t = min(EPI_T, n)
    nt = n // t
    row = lambda w: pl.BlockSpec((t, w), lambda i: (i, 0))
    full = lambda r, c: pl.BlockSpec((r, c), lambda i: (0, 0), pipeline_mode=pl.Buffered(1))
    return pl.pallas_call(
        functools.partial(_epilogue_kernel, alpha),
        grid=(nt,),
        in_specs=[
            row(on_a.shape[1]), row(on_b.shape[1]),
            pl.BlockSpec((None, t, d), lambda i: (3, i, 0)),
            pl.BlockSpec((None, t, d), lambda i: (4, i, 0)),
            pl.BlockSpec((None, t, d), lambda i: (5, i, 0)),
            row(d), row(d), row(ple),
            full(d, d), full(d, d), full(d, d), full(ple, d), full(1, d), full(1, d),
        ],
        out_specs=row(d),
        out_shape=jax.ShapeDtypeStruct((n, d), F32),
        compiler_params=pltpu.CompilerParams(
            dimension_semantics=("arbitrary",), vmem_limit_bytes=VMEM_LIMIT_BYTES),
        name="epilogue",
    )(on_a, on_b, proj, proj, proj, ya, x, p, w_proj_b, w_out, w_gate, w_ple, ln_g, ln_b)


def kernel(x, p, w_in, conv_w, conv_b, w_proj_a, lambda_q1, lambda_k1, lambda_q2, lambda_k2,
           subln_g, w_proj_b, w_out, w_ple, w_ple_gate, ln_g, ln_b):
    depth = w_in.shape[0]
    bsz, seq, d = x.shape
    alpha = (2.0 * depth) ** 0.25
    slopes = jnp.asarray(
        np.float32(2.0) ** (-8.0 * np.arange(1, N_HEADS + 1, dtype=np.float32) / N_HEADS))
    h = x
    for i in range(depth):
        lambda_init = 0.8 - 0.6 * math.exp(-0.3 * i)
        w_in_b = w_in[i].astype(BF16)
        proj = _proj(h.reshape(bsz * seq, d), w_in_b).reshape(N_PROJ + 1, bsz, seq, d)
        ya = _conv_branch(proj, w_in_b, conv_w[i], conv_b[i][None, :], w_proj_a[i].astype(BF16))
        on_a, on_b = _attention(proj, slopes, lambda_q1[i][None, :], lambda_k1[i][None, :],
                                lambda_q2[i][None, :], lambda_k2[i][None, :], subln_g[i][:, None],
                                lambda_init)
        out = _epilogue(on_a.reshape(bsz * seq, -1), on_b.reshape(bsz * seq, -1),
                        proj.reshape(N_PROJ + 1, bsz * seq, d),
                        ya.reshape(bsz * seq, d), h.reshape(bsz * seq, d),
                        p[i].reshape(bsz * seq, -1), w_proj_b[i].astype(BF16),
                        w_out[i].astype(BF16), w_ple_gate[i].astype(BF16),
                        w_ple[i].astype(BF16), ln_g[i][None, :], ln_b[i][None, :], alpha)
        h = out.reshape(bsz, seq, d)
    return h
```
